```python
import jax
import jax.numpy as jnp
from jax import lax
import numpy as np

D_MODEL = 1024
BATCH = 8
SEQ = 2048
DEPTH = 4

HGRN_HEADS = 4
HGRN_DK = 128
HGRN_DV = 128
HGRN_CHUNK = 64
HGRN_KW = HGRN_HEADS * HGRN_DK
HGRN_VW = HGRN_HEADS * HGRN_DV
MLA_HEADS = 8
MLA_Q_RANK = 384
MLA_KV_RANK = 256
MLA_NOPE = 64
MLA_ROPE = 32
MLA_DV = 64
MLA_VW = MLA_HEADS * MLA_DV
ROPE_THETA = 10000.0
Q_BLOCK = 128
N_EXPERTS = 32
TOP_K = 4
D_EXPERT = 1024
SWIGLU_LIMIT = 7.0
SWIGLU_ALPHA = 1.702
MOE_BLOCK = 256
N_BRANCHES = 2
NORM_EPS = 1e-6
IN_SPLITS = (HGRN_KW, HGRN_KW, HGRN_KW, HGRN_VW, HGRN_VW, MLA_Q_RANK, MLA_KV_RANK, MLA_ROPE, N_BRANCHES * D_MODEL)
IN_WIDTH = 3 * HGRN_KW + 2 * HGRN_VW + MLA_Q_RANK + MLA_KV_RANK + MLA_ROPE + N_BRANCHES * D_MODEL

kernel_name = "hybrid_hgrn2_mla_moe_adaln_encoder"


def rms_norm(x, gain):
    xf = x.astype(jnp.float32)
    y = xf * lax.rsqrt(jnp.mean(jnp.square(xf), axis=-1, keepdims=True) + NORM_EPS)
    return (y * gain.astype(jnp.float32)).astype(x.dtype)


def rope_tables(positions):
    inv_freq = ROPE_THETA ** (-jnp.arange(0, MLA_ROPE, 2, dtype=jnp.float32) / MLA_ROPE)
    ang = positions.astype(jnp.float32)[..., None] * inv_freq
    return jnp.cos(ang), jnp.sin(ang)


def apply_rope(x, cos, sin):
    half = x.shape[-1] // 2
    xf = x.astype(jnp.float32)
    x1, x2 = xf[..., :half], xf[..., half:]
    return jnp.concatenate([x1 * cos - x2 * sin, x2 * cos + x1 * sin], axis=-1).astype(x.dtype)


def hgrn2_chunk_scan(q, k, v, log_f):
    B, H, L, DK = q.shape
    DV = v.shape[-1]
    n = L // HGRN_CHUNK

    def to_chunks(a):
        return a.reshape(B, H, n, HGRN_CHUNK, a.shape[-1]).transpose(2, 0, 1, 3, 4)

    mask = jnp.tril(jnp.ones((HGRN_CHUNK, HGRN_CHUNK), dtype=bool))[:, :, None]

    def step(state, inp):
        q_c, k_c, v_c, g_c = inp
        b = jnp.cumsum(g_c, axis=-2)
        diff = b[..., :, None, :] - b[..., None, :, :]
        decay = jnp.exp(jnp.where(mask, diff, -jnp.inf))
        scores = jnp.einsum('bhtd,bhsd,bhtsd->bhts', q_c, k_c, decay)
        o = (jnp.einsum('bhts,bhsv->bhtv', scores, v_c)
             + jnp.einsum('bhtd,bhdv->bhtv', q_c * jnp.exp(b), state))
        b_last = b[..., -1:, :]
        state = (jnp.exp(b_last[..., 0, :])[..., None] * state
                 + jnp.einsum('bhsd,bhsv->bhdv', k_c * jnp.exp(b_last - b), v_c))
        return state, o

    s0 = jnp.zeros((B, H, DK, DV), jnp.float32)
    _, o = lax.scan(step, s0, (to_chunks(q), to_chunks(k), to_chunks(v), to_chunks(log_f)))
    return o.transpose(1, 2, 0, 3, 4).reshape(B, H, L, DV)


def hgrn2_branch(q_in, f_fw_in, f_bw_in, i_in, g_in, lb_fw, lb_bw, out_gain):
    B, S, _ = q_in.shape

    def heads(a, d):
        return a.reshape(B, S, HGRN_HEADS, d).transpose(0, 2, 1, 3).astype(jnp.float32)

    q = heads(jax.nn.silu(q_in.astype(jnp.float32)), HGRN_DK)
    v = heads(i_in, HGRN_DV)

    def forget(f_in, lb):
        f = lb + (1.0 - lb) * jax.nn.sigmoid(f_in.astype(jnp.float32))
        return heads(1.0 - f, HGRN_DK), heads(jnp.log(f), HGRN_DK)

    k_fw, lf_fw = forget(f_fw_in, lb_fw)
    k_bw, lf_bw = forget(f_bw_in, lb_bw)
    flip = lambda a: jnp.flip(a, axis=2)
    o = (hgrn2_chunk_scan(q, k_fw, v, lf_fw)
         + flip(hgrn2_chunk_scan(flip(q), flip(k_bw), flip(v), flip(lf_bw))))
    o = rms_norm(o, out_gain)
    o = o.transpose(0, 2, 1, 3).reshape(B, S, HGRN_VW)
    return (o * jax.nn.sigmoid(g_in.astype(jnp.float32))).astype(q_in.dtype)


def mla_branch(c_q, c_kv, k_pe, cos, sin, q_norm_gain, w_q_up, kv_norm_gain, w_kv_up):
    B, S, _ = c_q.shape
    q = (rms_norm(c_q, q_norm_gain) @ w_q_up).reshape(B, S, MLA_HEADS, MLA_NOPE + MLA_ROPE)
    q_nope, q_pe = q[..., :MLA_NOPE], q[..., MLA_NOPE:]
    q_pe = apply_rope(q_pe, cos[:, :, None, :], sin[:, :, None, :])
    kv = (rms_norm(c_kv, kv_norm_gain) @ w_kv_up).reshape(B, S, MLA_HEADS, MLA_NOPE + MLA_DV)
    k_nope, v = kv[..., :MLA_NOPE], kv[..., MLA_NOPE:]
    k_pe = apply_rope(k_pe, cos, sin)
    n_blocks = S // Q_BLOCK
    scale = (MLA_NOPE + MLA_ROPE) ** -0.5

    def to_blocks(a):
        return a.reshape(B, n_blocks, Q_BLOCK, *a.shape[2:]).swapaxes(0, 1)

    def attend(blk):
        qn, qp = blk
        s = (jnp.einsum('bqhd,bkhd->bhqk', qn, k_nope)
             + jnp.einsum('bqhr,bkr->bhqk', qp, k_pe))
        p = jax.nn.softmax(s.astype(jnp.float32) * scale, axis=-1)
        return jnp.einsum('bhqk,bkhv->bqhv', p.astype(v.dtype), v)

    o = lax.map(attend, (to_blocks(q_nope), to_blocks(q_pe)))
    return o.swapaxes(0, 1).reshape(B, S, MLA_VW)


def hybrid_mixer(h, cos, sin, w_in, lb_fw, lb_bw, hgrn_gain, w_out_hgrn,
                 q_norm_gain, w_q_up, kv_norm_gain, w_kv_up, w_out_mla, w_o):
    proj = h @ w_in
    split_at = np.cumsum(IN_SPLITS)[:-1].tolist()
    q_a, f_fw, f_bw, i_a, g_a, c_q, c_kv, k_pe, gates = jnp.split(proj, split_at, axis=-1)
    y_a = hgrn2_branch(q_a, f_fw, f_bw, i_a, g_a, lb_fw, lb_bw, hgrn_gain) @ w_out_hgrn
    y_b = mla_branch(c_q, c_kv, k_pe, cos, sin, q_norm_gain, w_q_up, kv_norm_gain, w_kv_up) @ w_out_mla
    gate_a, gate_b = jnp.split(gates, N_BRANCHES, axis=-1)
    y = jax.nn.sigmoid(gate_a) * y_a + jax.nn.sigmoid(gate_b) * y_b
    return y @ w_o


def moe_ffn(h, w_router, b_router, w_gate_up, b_gate_up, w_down, b_down):
    T, D = h.shape
    logits = (h @ w_router + b_router).astype(jnp.float32)
    top_val, top_idx = lax.top_k(logits, TOP_K)
    top_w = jax.nn.softmax(top_val, axis=-1)
    A = T * TOP_K
    flat_e = top_idx.reshape(A)
    order = jnp.argsort(flat_e)
    e_sorted = flat_e[order]
    tok_sorted = order // TOP_K
    w_sorted = top_w.reshape(A)[order]
    counts = jnp.bincount(flat_e, length=N_EXPERTS)
    padded = (counts + MOE_BLOCK - 1) // MOE_BLOCK * MOE_BLOCK
    pad_end = jnp.cumsum(padded)
    pad_start = pad_end - padded
    start = jnp.cumsum(counts) - counts
    dest = pad_start[e_sorted] + jnp.arange(A, dtype=jnp.int32) - start[e_sorted]
    n_blocks = -(-(A + N_EXPERTS * (MOE_BLOCK - 1)) // MOE_BLOCK)
    rows = n_blocks * MOE_BLOCK
    x_rows = jnp.zeros((rows, D), h.dtype).at[dest].set(h[tok_sorted])
    block_start = jnp.arange(n_blocks, dtype=jnp.int32) * MOE_BLOCK
    block_e = jnp.minimum(jnp.searchsorted(pad_end, block_start, side='right'), N_EXPERTS - 1)

    def expert_block(args):
        xb, e = args
        gu = xb @ w_gate_up[e] + b_gate_up[e]
        gate = jnp.minimum(gu[:, 0::2], SWIGLU_LIMIT)
        up = jnp.clip(gu[:, 1::2], -SWIGLU_LIMIT, SWIGLU_LIMIT)
        act = (up + 1.0) * gate * jax.nn.sigmoid(SWIGLU_ALPHA * gate)
        return act @ w_down[e] + b_down[e]

    y_rows = lax.map(expert_block, (x_rows.reshape(n_blocks, MOE_BLOCK, D), block_e)).reshape(rows, D)
    contrib = y_rows[dest].astype(jnp.float32) * w_sorted[:, None]
    out = jnp.zeros((T, D), jnp.float32).at[tok_sorted].add(contrib)
    return out.astype(h.dtype)


def setup_inputs(seed: int = 0) -> dict:
    key = jax.random.key(seed)
    ks = jax.random.split(key, 24)

    def nrm(k, shape, scale):
        return jax.random.normal(k, shape, jnp.float32) * scale

    D = D_MODEL
    return {
        "x": nrm(ks[0], (BATCH, SEQ, D), 1.0),
        "c": nrm(ks[1], (BATCH, D), 1.0),
        "positions": (jnp.arange(SEQ, dtype=jnp.int32)[None, :]
                      + jax.random.randint(ks[2], (BATCH, 1), 0, SEQ, dtype=jnp.int32)),
        "w_ada": nrm(ks[3], (DEPTH, D, 6 * D), 0.5 * D ** -0.5),
        "b_ada": nrm(ks[4], (DEPTH, 6 * D), 0.02),
        "norm_mix_gain": 1.0 + nrm(ks[5], (DEPTH, D), 0.02),
        "norm_ffn_gain": 1.0 + nrm(ks[6], (DEPTH, D), 0.02),
        "w_in": nrm(ks[7], (DEPTH, D, IN_WIDTH), D ** -0.5),
        "hgrn_lb_logits": nrm(ks[8], (DEPTH, 2, HGRN_KW), 0.5),
        "hgrn_out_norm_gain": 1.0 + nrm(ks[9], (DEPTH, HGRN_DV), 0.02),
        "w_out_hgrn": nrm(ks[10], (DEPTH, HGRN_VW, D), HGRN_VW ** -0.5),
        "mla_q_norm_gain": 1.0 + nrm(ks[11], (DEPTH, MLA_Q_RANK), 0.02),
        "w_q_up": nrm(ks[12], (DEPTH, MLA_Q_RANK, MLA_HEADS * (MLA_NOPE + MLA_ROPE)), MLA_Q_RANK ** -0.5),
        "mla_kv_norm_gain": 1.0 + nrm(ks[13], (DEPTH, MLA_KV_RANK), 0.02),
        "w_kv_up": nrm(ks[14], (DEPTH, MLA_KV_RANK, MLA_HEADS * (MLA_NOPE + MLA_DV)), MLA_KV_RANK ** -0.5),
        "w_out_mla": nrm(ks[15], (DEPTH, MLA_VW, D), MLA_VW ** -0.5),
        "w_o": nrm(ks[16], (DEPTH, D, D), D ** -0.5),
        "w_router": nrm(ks[17], (DEPTH, D, N_EXPERTS), D ** -0.5),
        "b_router": nrm(ks[18], (DEPTH, N_EXPERTS), 0.01),
        "w_gate_up": nrm(ks[19], (DEPTH, N_EXPERTS, D, 2 * D_EXPERT), D ** -0.5),
        "b_gate_up": nrm(ks[20], (DEPTH, N_EXPERTS, 2 * D_EXPERT), 0.02),
        "w_down": nrm(ks[21], (DEPTH, N_EXPERTS, D_EXPERT, D), D_EXPERT ** -0.5),
        "b_down": nrm(ks[22], (DEPTH, N_EXPERTS, D), 0.02),
        "final_norm_gain": 1.0 + nrm(ks[23], (D,), 0.02),
    }


def reference(x, c, positions, w_ada, b_ada, norm_mix_gain, norm_ffn_gain, w_in, hgrn_lb_logits,
              hgrn_out_norm_gain, w_out_hgrn, mla_q_norm_gain, w_q_up, mla_kv_norm_gain, w_kv_up,
              w_out_mla, w_o, w_router, b_router, w_gate_up, b_gate_up, w_down, b_down,
              final_norm_gain):
    B, S, D = x.shape
    cos, sin = rope_tables(positions)
    lb = jnp.cumsum(jax.nn.softmax(hgrn_lb_logits.astype(jnp.float32), axis=0), axis=0)
    lb = lb - lb[0]
    cond = jax.nn.silu(c)
    for l in range(DEPTH):
        mod = cond @ w_ada[l] + b_ada[l]
        sh1, sc1, g1, sh2, sc2, g2 = jnp.split(mod[:, None, :], 6, axis=-1)
        h = rms_norm(x, norm_mix_gain[l]) * (1.0 + sc1) + sh1
        x = x + g1 * hybrid_mixer(h, cos, sin, w_in[l], lb[l, 0], lb[l, 1], hgrn_out_norm_gain[l],
                                  w_out_hgrn[l], mla_q_norm_gain[l], w_q_up[l], mla_kv_norm_gain[l],
                                  w_kv_up[l], w_out_mla[l], w_o[l])
        h = rms_norm(x, norm_ffn_gain[l]) * (1.0 + sc2) + sh2
        x = x + g2 * moe_ffn(h.reshape(B * S, D), w_router[l], b_router[l], w_gate_up[l],
                             b_gate_up[l], w_down[l], b_down[l]).reshape(B, S, D)
    return rms_norm(x, final_norm_gain)
```

```python
import functools

import numpy as np
import jax
import jax.numpy as jnp
from jax import lax
from jax.experimental import pallas as pl
from jax.experimental.pallas import tpu as pltpu

F32 = jnp.float32
BF16 = jnp.bfloat16

D_MODEL = 1024
HG_HEADS = 4
HG_DK = 128
HG_W = HG_HEADS * HG_DK
HG_CHUNK = 64
MLA_HEADS = 8
Q_RANK = 384
KV_RANK = 256
NOPE = 64
ROPE = 32
MLA_DV = 64
MLA_VW = MLA_HEADS * MLA_DV
MLA_IN_W = 768
HEAD_LANES = 128
ROPE_THETA = 10000.0
N_EXPERTS = 32
TOP_K = 4
D_EXPERT = 1024
MOE_BLOCK = 256
SWIGLU_LIMIT = 7.0
SWIGLU_ALPHA = 1.702
NORM_EPS = 1e-6
LANES = 128
VMEM_LIMIT = 56 * 1024 * 1024

_NT = (((1,), (1,)), ((), ()))
_TN = (((0,), (0,)), ((), ()))


def _dot(a, b):
    return jnp.dot(a, b, preferred_element_type=F32)


def _dot_nt(a, b):
    return lax.dot_general(a, b, _NT, preferred_element_type=F32)


def _dot_tn(a, b):
    return lax.dot_general(a, b, _TN, preferred_element_type=F32)


def _sigmoid(x):
    return 1.0 / (1.0 + jnp.exp(-x))


def _split_bf16(x):
    hi = x.astype(BF16)
    lo = (x - hi.astype(F32)).astype(BF16)
    return hi, lo


def _params(*sem):
    return pltpu.CompilerParams(dimension_semantics=sem, vmem_limit_bytes=VMEM_LIMIT)


def _const_spec(shape):
    nd = len(shape)
    return pl.BlockSpec(shape, lambda *_: (0,) * nd)


def _ada_kernel(c_ref, w_ref, b_ref, o_ref):
    c = c_ref[...]
    cond = (c * _sigmoid(c)).astype(BF16)
    o_ref[...] = _dot(cond, w_ref[...].astype(BF16)) + b_ref[...]


def _ada_modulation(c, w_ada, b_ada):
    depth, d, n = w_ada.shape
    b = c.shape[0]
    tn = 1536
    return pl.pallas_call(
        _ada_kernel,
        grid=(depth, n // tn),
        in_specs=[
            pl.BlockSpec((b, d), lambda l, j: (0, 0)),
            pl.BlockSpec((None, d, tn), lambda l, j: (l, 0, j)),
            pl.BlockSpec((None, 1, tn), lambda l, j: (l, 0, j)),
        ],
        out_specs=pl.BlockSpec((None, b, tn), lambda l, j: (l, 0, j)),
        out_shape=jax.ShapeDtypeStruct((depth, b, n), F32),
        compiler_params=_params("arbitrary", "arbitrary"),
        name="ada_modulation",
    )(c, w_ada, b_ada.reshape(depth, 1, n))


def _lb_kernel(x_ref, o_ref):
    depth = x_ref.shape[0]
    x = [x_ref[l] for l in range(depth)]
    m = x[0]
    for l in range(1, depth):
        m = jnp.maximum(m, x[l])
    e = [jnp.exp(v - m) for v in x]
    tot = e[0]
    for l in range(1, depth):
        tot = tot + e[l]
    sm = [v / tot for v in e]
    acc = sm[0]
    o_ref[0] = acc - sm[0]
    for l in range(1, depth):
        acc = acc + sm[l]
        o_ref[l] = acc - sm[0]


def _hgrn_lower_bounds(lb_logits):
    return pl.pallas_call(
        _lb_kernel,
        out_shape=jax.ShapeDtypeStruct(lb_logits.shape, F32),
        name="hgrn_lower_bounds",
    )(lb_logits.astype(F32))


def _mix_in_kernel(x_ref, mod_ref, gain_ref, lb_ref, whg_ref, wml_ref, wgt_ref,
                   q_ref, lff_ref, lfb_ref, v_ref, gs_ref, m_ref, gt_ref):
    x = x_ref[...]
    ms = jnp.mean(x * x, axis=-1, keepdims=True)
    y = x * lax.rsqrt(ms + NORM_EPS) * gain_ref[...]
    h = y * (1.0 + mod_ref[1:2, :]) + mod_ref[0:1, :]
    hb = h.astype(BF16)

    def group(g):
        return _dot(hb, whg_ref[:, g * HG_W:(g + 1) * HG_W])

    qa = group(0)
    q_ref[...] = (qa * _sigmoid(qa)).astype(BF16)
    lbf = lb_ref[0:1, :]
    lff_ref[...] = jnp.log(lbf + (1.0 - lbf) * _sigmoid(group(1)))
    lbb = lb_ref[1:2, :]
    lfb_ref[...] = jnp.log(lbb + (1.0 - lbb) * _sigmoid(group(2)))
    v_ref[...] = group(3).astype(BF16)
    gs_ref[...] = _sigmoid(group(4)).astype(BF16)
    m_ref[...] = _dot(hb, wml_ref[...]).astype(BF16)
    for g in range(2 * D_MODEL // HG_W):
        sl = slice(g * HG_W, (g + 1) * HG_W)
        gt_ref[:, sl] = _sigmoid(_dot(hb, wgt_ref[:, sl])).astype(BF16)


def _mix_in(x, mod, gain, lb, whg, wml, wgt, seq):
    t, d = x.shape
    tm = min(512, seq)
    per_b = seq // tm
    row = lambda i: (i, 0)
    outs = [
        ((t, HG_W), BF16), ((t, HG_W), F32), ((t, HG_W), F32), ((t, HG_W), BF16),
        ((t, HG_W), BF16), ((t, MLA_IN_W), BF16), ((t, 2 * D_MODEL), BF16),
    ]
    return pl.pallas_call(
        _mix_in_kernel,
        grid=(t // tm,),
        in_specs=[
            pl.BlockSpec((tm, d), row),
            pl.BlockSpec((None, 6, d), lambda i: (i // per_b, 0, 0)),
            _const_spec((1, d)),
            _const_spec((2, HG_W)),
            _const_spec(whg.shape),
            _const_spec(wml.shape),
            _const_spec(wgt.shape),
        ],
        out_specs=[pl.BlockSpec((tm, s[1]), row) for s, _ in outs],
        out_shape=[jax.ShapeDtypeStruct(s, dt) for s, dt in outs],
        compiler_params=_params("arbitrary"),
        name="mix_in",
    )(x, mod, gain, lb, whg, wml, wgt)


_HG_LEVELS = (32, 16, 8, 4, 2, 1)


def _hgrn_constants():
    c = HG_CHUNK
    r = np.arange(c)

    def seg(lo, hi):
        return ((r >= lo) & (r <= hi)).astype(np.float32)

    zf = np.zeros((8, c, c), np.float32)
    zb = np.zeros((8, c, c), np.float32)
    masks = np.zeros((len(_HG_LEVELS), c, c), np.float32)
    for p in range(c):
        zf[0, p] = seg(0, p)
        zb[0, p] = seg(p, c - 1)
        zf[7, p] = seg(p + 1, c - 1)
        zb[7, p] = seg(0, p - 1)
    for li, m in enumerate(_HG_LEVELS):
        for p in range(c):
            base = (p // (2 * m)) * 2 * m
            later = (p - base) >= m
            last_early = base + m - 1
            first_late = base + m
            zf[1 + li, p] = seg(last_early + 1, p) if later else seg(p + 1, last_early)
            zb[1 + li, p] = seg(first_late, p - 1) if later else seg(p, first_late - 1)
        blk = r // (2 * m)
        late = (r % (2 * m)) >= m
        masks[li] = ((blk[:, None] == blk[None, :]) & late[:, None] & (~late)[None, :]).astype(np.float32)
    masks_all = np.concatenate([masks, masks.transpose(0, 2, 1)], axis=0)
    return zf.reshape(8 * c, c), zb.reshape(8 * c, c), masks_all


def _hgrn_chunk(lf, qf, vb, zl, mk_ref, mask_base, state_t, total_row):
    c = HG_CHUNK
    hi, lo = _split_bf16(lf)
    sums = _dot(zl, hi) + _dot(zl, lo)
    e = jnp.exp(sums)
    k = 1.0 - jnp.exp(lf)
    scores = jnp.zeros((c, c), F32)
    for li in range(len(_HG_LEVELS)):
        el = e[(li + 1) * c:(li + 2) * c]
        p = _dot_nt((qf * el).astype(BF16), (k * el).astype(BF16))
        scores = scores + jnp.where(mk_ref[mask_base + li] > 0.0, p, 0.0)
    diag = jnp.sum(qf * k, axis=1, keepdims=True)
    q_in = (qf * e[0:c]).astype(BF16)
    o = (_dot(scores.astype(BF16), vb) + _dot_nt(q_in, state_t.astype(BF16))
         + diag * vb.astype(F32))
    k_out = (k * e[7 * c:8 * c]).astype(BF16)
    new_state_t = e[total_row:total_row + 1] * state_t + _dot_tn(vb, k_out)
    return o, new_state_t


def _hgrn_kernel(q_ref, lff_ref, lfb_ref, v_ref, gs_ref, gain_ref, zf_ref, zb_ref, mk_ref,
                 o_ref, of_sc, ob_sc):
    c = HG_CHUNK
    n_chunks = q_ref.shape[0] // c
    n_lv = len(_HG_LEVELS)

    def body(n, carry):
        sf, sb = carry
        rf = pl.multiple_of(n * c, c)
        rb = pl.multiple_of((n_chunks - 1 - n) * c, c)
        o_f, sf = _hgrn_chunk(lff_ref[pl.ds(rf, c), :], q_ref[pl.ds(rf, c), :].astype(F32),
                              v_ref[pl.ds(rf, c), :], zf_ref[...], mk_ref, 0, sf, c - 1)
        of_sc[pl.ds(rf, c), :] = o_f
        o_b, sb = _hgrn_chunk(lfb_ref[pl.ds(rb, c), :], q_ref[pl.ds(rb, c), :].astype(F32),
                              v_ref[pl.ds(rb, c), :], zb_ref[...], mk_ref, n_lv, sb, 0)
        ob_sc[pl.ds(rb, c), :] = o_b
        return sf, sb

    zero = jnp.zeros((HG_DK, HG_DK), F32)
    lax.fori_loop(0, n_chunks, body, (zero, zero))
    o = of_sc[...] + ob_sc[...]
    ms = jnp.mean(o * o, axis=-1, keepdims=True)
    y = o * lax.rsqrt(ms + NORM_EPS) * gain_ref[...]
    o_ref[...] = (y * gs_ref[...].astype(F32)).astype(o_ref.dtype)


def _hgrn(q, lff, lfb, v, gs, gain, batch, seq):
    zf, zb, masks = _hgrn_constants()
    t = q.shape[0]
    blk = pl.BlockSpec((seq, HG_DK), lambda b, h: (b, h))
    return pl.pallas_call(
        _hgrn_kernel,
        grid=(batch, HG_HEADS),
        in_specs=[blk, blk, blk, blk, blk, _const_spec((1, HG_DK)),
                  _const_spec(zf.shape), _const_spec(zb.shape), _const_spec(masks.shape)],
        out_specs=blk,
        out_shape=jax.ShapeDtypeStruct((t, HG_W), BF16),
        scratch_shapes=[pltpu.VMEM((seq, HG_DK), F32), pltpu.VMEM((seq, HG_DK), F32)],
        compiler_params=_params("arbitrary", "arbitrary"),
        name="hgrn",
    )(q, lff, lfb, v, gs, gain, jnp.asarray(zf, BF16), jnp.asarray(zb, BF16), jnp.asarray(masks))


def _mla_prep_kernel(m_ref, cos_ref, sin_ref, gq_ref, gkv_ref, wq1_ref, wq2_ref, wk_ref, wv_ref,
                     e1_ref, e2_ref, q_ref, k_ref, v_ref):
    m = m_ref[...]
    cq = m[:, 0:Q_RANK].astype(F32)
    nq = (cq * lax.rsqrt(jnp.mean(cq * cq, axis=-1, keepdims=True) + NORM_EPS) * gq_ref[...]).astype(BF16)
    ckv = m[:, Q_RANK:Q_RANK + KV_RANK].astype(F32)
    nkv = (ckv * lax.rsqrt(jnp.mean(ckv * ckv, axis=-1, keepdims=True) + NORM_EPS) * gkv_ref[...]).astype(BF16)
    kpe = m[:, Q_RANK + KV_RANK:MLA_IN_W]
    cos = jnp.concatenate([cos_ref[...]] * MLA_HEADS, axis=1)
    sin = jnp.concatenate([sin_ref[...]] * MLA_HEADS, axis=1)
    scale = (NOPE + ROPE) ** -0.5
    q = _dot(nq, wq1_ref[...]) * cos + _dot(nq, wq2_ref[...]) * sin
    q_ref[...] = (q * scale).astype(BF16)
    k = _dot(nkv, wk_ref[...]) + _dot(kpe, e1_ref[...]) * cos + _dot(kpe, e2_ref[...]) * sin
    k_ref[...] = k.astype(BF16)
    v_ref[...] = _dot(nkv, wv_ref[...]).astype(BF16)


def _mla_prep(m, cos_t, sin_t, gq, gkv, wq1, wq2, wk, wv, e1, e2, seq):
    t = m.shape[0]
    tm = min(512, seq)
    row = lambda i: (i, 0)
    qk_w = MLA_HEADS * HEAD_LANES
    return pl.pallas_call(
        _mla_prep_kernel,
        grid=(t // tm,),
        in_specs=[
            pl.BlockSpec((tm, MLA_IN_W), row),
            pl.BlockSpec((tm, HEAD_LANES), row),
            pl.BlockSpec((tm, HEAD_LANES), row),
            _const_spec((1, Q_RANK)), _const_spec((1, KV_RANK)),
            _const_spec(wq1.shape), _const_spec(wq2.shape), _const_spec(wk.shape),
            _const_spec(wv.shape), _const_spec(e1.shape), _const_spec(e2.shape),
        ],
        out_specs=[pl.BlockSpec((tm, qk_w), row), pl.BlockSpec((tm, qk_w), row),
                   pl.BlockSpec((tm, MLA_VW), row)],
        out_shape=[jax.ShapeDtypeStruct((t, qk_w), BF16), jax.ShapeDtypeStruct((t, qk_w), BF16),
                   jax.ShapeDtypeStruct((t, MLA_VW), BF16)],
        compiler_params=_params("arbitrary"),
        name="mla_prep",
    )(m, cos_t, sin_t, gq, gkv, wq1, wq2, wk, wv, e1, e2)


def _mla_attn_kernel(q_ref, k_ref, v_ref, o_ref):
    outs = []
    for h in range(MLA_HEADS):
        qh = q_ref[:, h * HEAD_LANES:(h + 1) * HEAD_LANES]
        kh = k_ref[:, h * HEAD_LANES:(h + 1) * HEAD_LANES]
        s = _dot_nt(qh, kh)
        p = jnp.exp(s - jnp.max(s, axis=-1, keepdims=True))
        denom = jnp.sum(p, axis=-1, keepdims=True)
        vh = v_ref[:, h * MLA_DV:(h + 1) * MLA_DV]
        outs.append(_dot(p.astype(BF16), vh) / denom)
    o_ref[...] = jnp.concatenate(outs, axis=1).astype(o_ref.dtype)


def _mla_attn(q, k, v, batch, seq):
    t = q.shape[0]
    tq = min(256, seq)
    per_b = seq // tq
    qk_w = MLA_HEADS * HEAD_LANES
    return pl.pallas_call(
        _mla_attn_kernel,
        grid=(batch, per_b),
        in_specs=[
            pl.BlockSpec((tq, qk_w), lambda b, i: (b * per_b + i, 0)),
            pl.BlockSpec((seq, qk_w), lambda b, i: (b, 0)),
            pl.BlockSpec((seq, MLA_VW), lambda b, i: (b, 0)),
        ],
        out_specs=pl.BlockSpec((tq, MLA_VW), lambda b, i: (b * per_b + i, 0)),
        out_shape=jax.ShapeDtypeStruct((t, MLA_VW), BF16),
        compiler_params=_params("arbitrary", "arbitrary"),
        name="mla_attn",
    )(q, k, v)


def _mix_out_kernel(ho_ref, mo_ref, gt_ref, x_ref, mod_ref, woh_ref, wom_ref, wo_ref, o_ref):
    ya = _dot(ho_ref[...], woh_ref[...])
    yb = _dot(mo_ref[...], wom_ref[...])
    y = gt_ref[:, 0:D_MODEL].astype(F32) * ya + gt_ref[:, D_MODEL:2 * D_MODEL].astype(F32) * yb
    o_ref[...] = x_ref[...] + mod_ref[2:3, :] * _dot(y.astype(BF16), wo_ref[...])


def _mix_out(ho, mo, gt, x, mod, woh, wom, wo, seq):
    t, d = x.shape
    tm = min(512, seq)
    per_b = seq // tm
    row = lambda i: (i, 0)
    return pl.pallas_call(
        _mix_out_kernel,
        grid=(t // tm,),
        in_specs=[
            pl.BlockSpec((tm, HG_W), row), pl.BlockSpec((tm, MLA_VW), row),
            pl.BlockSpec((tm, 2 * d), row), pl.BlockSpec((tm, d), row),
            pl.BlockSpec((None, 6, d), lambda i: (i // per_b, 0, 0)),
            _const_spec(woh.shape), _const_spec(wom.shape), _const_spec(wo.shape),
        ],
        out_specs=pl.BlockSpec((tm, d), row),
        out_shape=jax.ShapeDtypeStruct((t, d), F32),
        compiler_params=_params("arbitrary"),
        name="mix_out",
    )(ho, mo, gt, x, mod, woh, wom, wo)


def _moe_num_blocks(tokens):
    return -(-(tokens * TOP_K + N_EXPERTS * (MOE_BLOCK - 1)) // MOE_BLOCK)


def _ffn_prep_kernel(x_ref, mod_ref, gain_ref, wrh_ref, wrl_ref, br_ref, tri_ref, upper_ref,
                     h_ref, idx_ref, w_ref, rank_ref, meta_ref, be_ref, count_sc):
    i = pl.program_id(0)
    tm = x_ref.shape[0]

    @pl.when(i == 0)
    def _():
        count_sc[...] = jnp.zeros_like(count_sc)

    x = x_ref[...]
    ms = jnp.mean(x * x, axis=-1, keepdims=True)
    y = x * lax.rsqrt(ms + NORM_EPS) * gain_ref[...]
    h = y * (1.0 + mod_ref[4:5, :]) + mod_ref[3:4, :]
    h_ref[...] = h

    h_hi, h_lo = _split_bf16(h)
    logits = (_dot(h_hi, wrh_ref[...]) + _dot(h_hi, wrl_ref[...]) + _dot(h_lo, wrh_ref[...])
              + br_ref[...])
    lane = lax.broadcasted_iota(jnp.int32, (tm, LANES), 1)
    neg = jnp.float32(-jnp.inf)
    cur = jnp.where(lane < N_EXPERTS, logits, neg)
    vals, ids = [], []
    for _ in range(TOP_K):
        mx = jnp.max(cur, axis=-1, keepdims=True)
        sel = jnp.min(jnp.where(cur == mx, lane, LANES), axis=-1, keepdims=True)
        vals.append(mx)
        ids.append(sel)
        cur = jnp.where(lane == sel, neg, cur)
    ex = [jnp.exp(v - vals[0]) for v in vals]
    tot = ex[0] + ex[1] + ex[2] + ex[3]

    onehot = jnp.zeros((tm, LANES), F32)
    for sel in ids:
        onehot = onehot + jnp.where(lane == sel, 1.0, 0.0)
    before = _dot(tri_ref[...], onehot.astype(BF16)) + count_sc[0:1, :]
    idx_out = jnp.zeros((tm, LANES), jnp.int32)
    w_out = jnp.zeros((tm, LANES), F32)
    rank_out = jnp.zeros((tm, LANES), jnp.int32)
    for kk in range(TOP_K):
        rk = jnp.sum(jnp.where(lane == ids[kk], before, 0.0), axis=-1, keepdims=True)
        idx_out = jnp.where(lane == kk, ids[kk], idx_out)
        w_out = jnp.where(lane == kk, ex[kk] / tot, w_out)
        rank_out = jnp.where(lane == kk, rk.astype(jnp.int32), rank_out)
    idx_ref[...] = idx_out[:, 0:TOP_K]
    w_ref[...] = w_out[:, 0:TOP_K]
    rank_ref[...] = rank_out[:, 0:TOP_K]
    count_sc[...] = count_sc[...] + jnp.sum(onehot, axis=0, keepdims=True)

    @pl.when(i == pl.num_programs(0) - 1)
    def _():
        counts = count_sc[...]
        blocks = jnp.floor((counts + (MOE_BLOCK - 1.0)) / MOE_BLOCK)
        b_hi = jnp.floor(blocks / 16.0)
        b_lo = blocks - 16.0 * b_hi
        up = upper_ref[...]
        rows16 = lambda a: jnp.concatenate([a, a], axis=0).astype(BF16)
        end_blocks = 16.0 * _dot(rows16(b_hi), up) + _dot(rows16(b_lo), up)
        pad_end = end_blocks[0:8] * MOE_BLOCK
        pad_start = pad_end - blocks * MOE_BLOCK
        row8 = lax.broadcasted_iota(jnp.int32, (8, LANES), 0)
        meta = jnp.where(row8 == 0, pad_start, jnp.where(row8 == 1, pad_end, counts))
        meta_ref[...] = meta.astype(jnp.int32)
        nb = be_ref.shape[0]
        starts = (lax.broadcasted_iota(jnp.int32, (nb, LANES), 0) * MOE_BLOCK).astype(F32)
        lane_b = lax.broadcasted_iota(jnp.int32, (nb, LANES), 1)
        passed = jnp.where((pad_end[0:1, :] <= starts) & (lane_b < N_EXPERTS), 1.0, 0.0)
        e_of_block = jnp.minimum(jnp.sum(passed, axis=-1, keepdims=True), N_EXPERTS - 1.0)
        be_ref[...] = jnp.broadcast_to(e_of_block, (nb, LANES)).astype(jnp.int32)


def _ffn_prep(x, mod, gain, wr_hi, wr_lo, br, seq):
    t, d = x.shape
    tm = min(512, seq)
    per_b = seq // tm
    nb = _moe_num_blocks(t)
    nb_pad = -(-nb // 8) * 8
    tri = jnp.asarray(np.tril(np.ones((tm, tm), np.float32), -1), BF16)
    upper = jnp.asarray(np.triu(np.ones((LANES, LANES), np.float32)), BF16)
    row = lambda i: (i, 0)
    return pl.pallas_call(
        _ffn_prep_kernel,
        grid=(t // tm,),
        in_specs=[
            pl.BlockSpec((tm, d), row),
            pl.BlockSpec((None, 6, d), lambda i: (i // per_b, 0, 0)),
            _const_spec((1, d)), _const_spec(wr_hi.shape), _const_spec(wr_lo.shape),
            _const_spec((1, LANES)), _const_spec((tm, tm)), _const_spec((LANES, LANES)),
        ],
        out_specs=[
            pl.BlockSpec((tm, d), row), pl.BlockSpec((tm, TOP_K), row),
            pl.BlockSpec((tm, TOP_K), row), pl.BlockSpec((tm, TOP_K), row),
            _const_spec((8, LANES)), _const_spec((nb_pad, LANES)),
        ],
        out_shape=[
            jax.ShapeDtypeStruct((t, d), F32), jax.ShapeDtypeStruct((t, TOP_K), jnp.int32),
            jax.ShapeDtypeStruct((t, TOP_K), F32), jax.ShapeDtypeStruct((t, TOP_K), jnp.int32),
            jax.ShapeDtypeStruct((8, LANES), jnp.int32), jax.ShapeDtypeStruct((nb_pad, LANES), jnp.int32),
        ],
        scratch_shapes=[pltpu.VMEM((8, LANES), F32)],
        compiler_params=_params("arbitrary"),
        name="ffn_prep",
    )(x, mod, gain, wr_hi, wr_lo, br, tri, upper)


def _row_copy(src, src_row, dst, dst_row, sem):
    return pltpu.make_async_copy(src.at[pl.ds(src_row, 1)], dst.at[pl.ds(dst_row, 1)], sem)


def _dispatch_kernel(idx_ref, rank_ref, start_ref, h_ref, rows_in, rows_out, sem):
    del rows_in
    tm = h_ref.shape[0]

    def issue(a, carry):
        dst = start_ref[idx_ref[a]] + rank_ref[a]
        _row_copy(h_ref, a // TOP_K, rows_out, dst, sem).start()
        return carry

    lax.fori_loop(0, tm * TOP_K, issue, 0)

    def drain(a, carry):
        _row_copy(h_ref, 0, rows_out, 0, sem).wait()
        return carry

    lax.fori_loop(0, tm * TOP_K, drain, 0)


def _dispatch(idx_flat, rank_flat, pad_start, h, n_rows):
    t, d = h.shape
    tm = min(256, t)
    na = tm * TOP_K
    smem = functools.partial(pl.BlockSpec, memory_space=pltpu.SMEM)
    return pl.pallas_call(
        _dispatch_kernel,
        grid=(t // tm,),
        in_specs=[
            smem((na,), lambda i: (i,)), smem((na,), lambda i: (i,)),
            smem((N_EXPERTS,), lambda i: (0,)),
            pl.BlockSpec((tm, d), lambda i: (i, 0)),
            pl.BlockSpec(memory_space=pl.ANY),
        ],
        out_specs=pl.BlockSpec(memory_space=pl.ANY),
        out_shape=jax.ShapeDtypeStruct((n_rows, d), F32),
        scratch_shapes=[pltpu.SemaphoreType.DMA(())],
        input_output_aliases={4: 0},
        compiler_params=_params("arbitrary"),
        name="moe_dispatch",
    )(idx_flat, rank_flat, pad_start, h, jnp.zeros((n_rows, d), F32))


def _experts_kernel(be_ref, nact_ref, x_ref, wgu_ref, bgu_ref, wdn_ref, bdn_ref, o_ref):
    del be_ref
    active = pl.program_id(0) < nact_ref[0]

    @pl.when(jnp.logical_not(active))
    def _():
        o_ref[...] = jnp.zeros_like(o_ref)

    @pl.when(active)
    def _():
        gu = _dot(x_ref[...].astype(BF16), wgu_ref[...]) + bgu_ref[...]
        gate = jnp.minimum(gu[:, 0:D_EXPERT], SWIGLU_LIMIT)
        up = jnp.clip(gu[:, D_EXPERT:2 * D_EXPERT], -SWIGLU_LIMIT, SWIGLU_LIMIT)
        act = (up + 1.0) * gate * _sigmoid(SWIGLU_ALPHA * gate)
        o_ref[...] = _dot(act.astype(BF16), wdn_ref[...]) + bdn_ref[...]


def _experts(block_e, n_active, x_rows, wgu, bgu, wdn, bdn, layer):
    n_rows, d = x_rows.shape
    nb = n_rows // MOE_BLOCK

    def row_map(i, be, na):
        return (jnp.minimum(i, na[0] - 1), 0)

    def w_map(i, be, na):
        return (layer, be[i], 0, 0)

    grid_spec = pltpu.PrefetchScalarGridSpec(
        num_scalar_prefetch=2,
        grid=(nb,),
        in_specs=[
            pl.BlockSpec((MOE_BLOCK, d), row_map),
            pl.BlockSpec((None, None, d, 2 * D_EXPERT), w_map),
            pl.BlockSpec((None, None, 1, 2 * D_EXPERT), w_map),
            pl.BlockSpec((None, None, D_EXPERT, d), w_map),
            pl.BlockSpec((None, None, 1, d), w_map),
        ],
        out_specs=pl.BlockSpec((MOE_BLOCK, d), lambda i, be, na: (i, 0)),
    )
    return pl.pallas_call(
        _experts_kernel,
        grid_spec=grid_spec,
        out_shape=jax.ShapeDtypeStruct((n_rows, d), F32),
        compiler_params=_params("arbitrary"),
        name="moe_experts",
    )(block_e, n_active, x_rows, wgu, bgu, wdn, bdn)


def _combine_kernel(idx_ref, rank_ref, start_ref, y_hbm, w_ref, x_ref, mod_ref, fin_ref,
                    o_ref, buf, sem, *, final_norm):
    tm = x_ref.shape[0]

    def issue(a, carry):
        src = start_ref[idx_ref[a]] + rank_ref[a]
        _row_copy(y_hbm, src, buf.at[a % TOP_K], a // TOP_K, sem).start()
        return carry

    lax.fori_loop(0, tm * TOP_K, issue, 0)

    def drain(a, carry):
        _row_copy(y_hbm, 0, buf.at[0], 0, sem).wait()
        return carry

    lax.fori_loop(0, tm * TOP_K, drain, 0)

    w = w_ref[...]
    moe = w[:, 0:1] * buf[0]
    for kk in range(1, TOP_K):
        moe = moe + w[:, kk:kk + 1] * buf[kk]
    out = x_ref[...] + mod_ref[5:6, :] * moe
    if final_norm:
        ms = jnp.mean(out * out, axis=-1, keepdims=True)
        out = out * lax.rsqrt(ms + NORM_EPS) * fin_ref[...]
    o_ref[...] = out


def _combine(idx_flat, rank_flat, pad_start, y_rows, w, x, mod, fin_gain, seq, final_norm):
    t, d = x.shape
    tm = min(256, seq)
    per_b = seq // tm
    na = tm * TOP_K
    smem = functools.partial(pl.BlockSpec, memory_space=pltpu.SMEM)
    row = lambda i: (i, 0)
    return pl.pallas_call(
        functools.partial(_combine_kernel, final_norm=final_norm),
        grid=(t // tm,),
        in_specs=[
            smem((na,), lambda i: (i,)), smem((na,), lambda i: (i,)),
            smem((N_EXPERTS,), lambda i: (0,)),
            pl.BlockSpec(memory_space=pl.ANY),
            pl.BlockSpec((tm, TOP_K), row), pl.BlockSpec((tm, d), row),
            pl.BlockSpec((None, 6, d), lambda i: (i // per_b, 0, 0)),
            _const_spec((1, d)),
        ],
        out_specs=pl.BlockSpec((tm, d), row),
        out_shape=jax.ShapeDtypeStruct((t, d), F32),
        scratch_shapes=[pltpu.VMEM((TOP_K, tm, d), F32), pltpu.SemaphoreType.DMA(())],
        compiler_params=_params("arbitrary"),
        name="moe_combine",
    )(idx_flat, rank_flat, pad_start, y_rows, w, x, mod, fin_gain)


def _rope_tables(positions):
    inv_freq = ROPE_THETA ** (-jnp.arange(0, ROPE, 2, dtype=F32) / ROPE)
    ang = positions.astype(F32).reshape(-1)[:, None] * inv_freq
    cos, sin = jnp.cos(ang), jnp.sin(ang)
    t = cos.shape[0]
    pad = jnp.zeros((t, HEAD_LANES - NOPE - ROPE), F32)
    cos_t = jnp.concatenate([jnp.ones((t, NOPE), F32), cos, cos, pad], axis=1)
    sin_t = jnp.concatenate([jnp.zeros((t, NOPE), F32), sin, sin, pad], axis=1)
    return cos_t, sin_t


def _rotate_half_matrix():
    half = ROPE // 2
    r = np.zeros((ROPE, ROPE), np.float32)
    for i in range(half):
        r[half + i, i] = -1.0
        r[i, half + i] = 1.0
    return r


def _mla_weights(w_q_up, w_kv_up):
    rot = jnp.asarray(_rotate_half_matrix())
    wq = w_q_up.reshape(Q_RANK, MLA_HEADS, NOPE + ROPE)
    zq = jnp.zeros((Q_RANK, MLA_HEADS, HEAD_LANES - NOPE - ROPE), F32)
    wq1 = jnp.concatenate([wq, zq], axis=2)
    wq_rot = jnp.einsum("rhd,de->rhe", wq[:, :, NOPE:], rot)
    wq2 = jnp.concatenate([jnp.zeros((Q_RANK, MLA_HEADS, NOPE), F32), wq_rot, zq], axis=2)
    wkv = w_kv_up.reshape(KV_RANK, MLA_HEADS, NOPE + MLA_DV)
    wk = jnp.concatenate([wkv[:, :, :NOPE], jnp.zeros((KV_RANK, MLA_HEADS, HEAD_LANES - NOPE), F32)], axis=2)
    wv = wkv[:, :, NOPE:]
    qk_w = MLA_HEADS * HEAD_LANES
    return (wq1.reshape(Q_RANK, qk_w).astype(BF16), wq2.reshape(Q_RANK, qk_w).astype(BF16),
            wk.reshape(KV_RANK, qk_w).astype(BF16), wv.reshape(KV_RANK, MLA_VW).astype(BF16))


def _rope_placement():
    rot = _rotate_half_matrix()
    e1 = np.zeros((LANES, MLA_HEADS, HEAD_LANES), np.float32)
    e2 = np.zeros((LANES, MLA_HEADS, HEAD_LANES), np.float32)
    for h in range(MLA_HEADS):
        e1[:ROPE, h, NOPE:NOPE + ROPE] = np.eye(ROPE, dtype=np.float32)
        e2[:ROPE, h, NOPE:NOPE + ROPE] = rot
    shape = (LANES, MLA_HEADS * HEAD_LANES)
    return jnp.asarray(e1.reshape(shape), BF16), jnp.asarray(e2.reshape(shape), BF16)


def _in_proj_weights(w_in_l):
    hg = w_in_l[:, 0:5 * HG_W]
    o = 5 * HG_W
    ml = w_in_l[:, o:o + Q_RANK + KV_RANK + ROPE]
    ml = jnp.concatenate([ml, jnp.zeros((D_MODEL, MLA_IN_W - ml.shape[1]), F32)], axis=1)
    gt = w_in_l[:, o + Q_RANK + KV_RANK + ROPE:]
    return hg.astype(BF16), ml.astype(BF16), gt.astype(BF16)


def kernel(x, c, positions, w_ada, b_ada, norm_mix_gain, norm_ffn_gain, w_in, hgrn_lb_logits,
           hgrn_out_norm_gain, w_out_hgrn, mla_q_norm_gain, w_q_up, mla_kv_norm_gain, w_kv_up,
           w_out_mla, w_o, w_router, b_router, w_gate_up, b_gate_up, w_down, b_down,
           final_norm_gain):
    batch, seq, d = x.shape
    depth = w_ada.shape[0]
    t = batch * seq
    n_rows = _moe_num_blocks(t) * MOE_BLOCK

    mod_all = _ada_modulation(c, w_ada, b_ada).reshape(depth, batch, 6, d)
    lb_all = _hgrn_lower_bounds(hgrn_lb_logits)
    cos_t, sin_t = _rope_tables(positions)
    e1, e2 = _rope_placement()

    wgu = jnp.concatenate([w_gate_up[..., 0::2], w_gate_up[..., 1::2]], axis=-1).astype(BF16)
    bgu = jnp.concatenate([b_gate_up[..., 0::2], b_gate_up[..., 1::2]], axis=-1)[:, :, None, :]
    wdn = w_down.astype(BF16)
    bdn = b_down[:, :, None, :]
    wr_pad = jnp.pad(w_router, ((0, 0), (0, 0), (0, LANES - N_EXPERTS)))
    br_pad = jnp.pad(b_router, ((0, 0), (0, LANES - N_EXPERTS)))

    xs = x.reshape(t, d)
    for l in range(depth):
        mod = mod_all[l]
        whg, wml, wgt = _in_proj_weights(w_in[l])
        q, lff, lfb, v, gs, m, gt = _mix_in(xs, mod, norm_mix_gain[l][None, :], lb_all[l], whg, wml, wgt, seq)
        ho = _hgrn(q, lff, lfb, v, gs, hgrn_out_norm_gain[l][None, :], batch, seq)
        wq1, wq2, wk, wv = _mla_weights(w_q_up[l], w_kv_up[l])
        qh, kh, vh = _mla_prep(m, cos_t, sin_t, mla_q_norm_gain[l][None, :], mla_kv_norm_gain[l][None, :],
                               wq1, wq2, wk, wv, e1, e2, seq)
        mo = _mla_attn(qh, kh, vh, batch, seq)
        xs = _mix_out(ho, mo, gt, xs, mod, w_out_hgrn[l].astype(BF16), w_out_mla[l].astype(BF16),
                      w_o[l].astype(BF16), seq)

        wr_hi = wr_pad[l].astype(BF16)
        wr_lo = (wr_pad[l] - wr_hi.astype(F32)).astype(BF16)
        h2, idx, tw, rank, meta, block_e = _ffn_prep(xs, mod, norm_ffn_gain[l][None, :], wr_hi, wr_lo,
                                                     br_pad[l][None, :], seq)
        pad_start = meta[0, :N_EXPERTS]
        n_active = meta[1, N_EXPERTS - 1:N_EXPERTS] // MOE_BLOCK
        idx_flat = idx.reshape(-1)
        rank_flat = rank.reshape(-1)
        x_rows = _dispatch(idx_flat, rank_flat, pad_start, h2, n_rows)
        y_rows = _experts(block_e[:n_rows // MOE_BLOCK, 0], n_active, x_rows, wgu, bgu, wdn, bdn, l)
        xs = _combine(idx_flat, rank_flat, pad_start, y_rows, tw, xs, mod, final_norm_gain[None, :],
                      seq, final_norm=(l == depth - 1))
    return xs.reshape(batch, seq, d)
```

```python
import functools

import numpy as np
import jax
import jax.numpy as jnp
from jax import lax
from jax.experimental import pallas as pl
from jax.experimental.pallas import tpu as pltpu

F32 = jnp.float32
BF16 = jnp.bfloat16

D_MODEL = 1024
HG_HEADS = 4
HG_DK = 128
HG_W = HG_HEADS * HG_DK
HG_CHUNK = 64
MLA_HEADS = 8
Q_RANK = 384
KV_RANK = 256
NOPE = 64
ROPE = 32
MLA_DV = 64
MLA_VW = MLA_HEADS * MLA_DV
MLA_IN_W = 768
HEAD_LANES = 128
ROPE_THETA = 10000.0
N_EXPERTS = 32
TOP_K = 4
D_EXPERT = 1024
MOE_BLOCK = 256
SWIGLU_LIMIT = 7.0
SWIGLU_ALPHA = 1.702
NORM_EPS = 1e-6
LANES = 128
VMEM_LIMIT = 56 * 1024 * 1024

_NT = (((1,), (1,)), ((), ()))
_TN = (((0,), (0,)), ((), ()))


def _dot(a, b):
    return jnp.dot(a, b, preferred_element_type=F32)


def _dot_nt(a, b):
    return lax.dot_general(a, b, _NT, preferred_element_type=F32)


def _dot_tn(a, b):
    return lax.dot_general(a, b, _TN, preferred_element_type=F32)


def _sigmoid(x):
    return 1.0 / (1.0 + jnp.exp(-x))


def _split_bf16(x):
    hi = x.astype(BF16)
    lo = (x - hi.astype(F32)).astype(BF16)
    return hi, lo


def _params(*sem):
    return pltpu.CompilerParams(dimension_semantics=sem, vmem_limit_bytes=VMEM_LIMIT)


def _const_spec(shape):
    nd = len(shape)
    return pl.BlockSpec(shape, lambda *_: (0,) * nd)


def _ada_kernel(c_ref, w_ref, b_ref, o_ref):
    c = c_ref[...]
    cond = (c * _sigmoid(c)).astype(BF16)
    o_ref[...] = _dot(cond, w_ref[...].astype(BF16)) + b_ref[...]


def _ada_modulation(c, w_ada, b_ada):
    depth, d, n = w_ada.shape
    b = c.shape[0]
    tn = 1536
    return pl.pallas_call(
        _ada_kernel,
        grid=(depth, n // tn),
        in_specs=[
            pl.BlockSpec((b, d), lambda l, j: (0, 0)),
            pl.BlockSpec((None, d, tn), lambda l, j: (l, 0, j)),
            pl.BlockSpec((None, 1, tn), lambda l, j: (l, 0, j)),
        ],
        out_specs=pl.BlockSpec((None, b, tn), lambda l, j: (l, 0, j)),
        out_shape=jax.ShapeDtypeStruct((depth, b, n), F32),
        compiler_params=_params("arbitrary", "arbitrary"),
        name="ada_modulation",
    )(c, w_ada, b_ada.reshape(depth, 1, n))


def _lb_kernel(x_ref, o_ref):
    depth = x_ref.shape[0]
    x = [x_ref[l] for l in range(depth)]
    m = x[0]
    for l in range(1, depth):
        m = jnp.maximum(m, x[l])
    e = [jnp.exp(v - m) for v in x]
    tot = e[0]
    for l in range(1, depth):
        tot = tot + e[l]
    sm = [v / tot for v in e]
    acc = sm[0]
    o_ref[0] = acc - sm[0]
    for l in range(1, depth):
        acc = acc + sm[l]
        o_ref[l] = acc - sm[0]


def _hgrn_lower_bounds(lb_logits):
    return pl.pallas_call(
        _lb_kernel,
        out_shape=jax.ShapeDtypeStruct(lb_logits.shape, F32),
        name="hgrn_lower_bounds",
    )(lb_logits.astype(F32))


def _mix_in_kernel(x_ref, mod_ref, gain_ref, lb_ref, whg_ref, wml_ref, wgt_ref,
                   q_ref, lff_ref, lfb_ref, v_ref, gs_ref, m_ref, gt_ref):
    x = x_ref[...]
    ms = jnp.mean(x * x, axis=-1, keepdims=True)
    y = x * lax.rsqrt(ms + NORM_EPS) * gain_ref[...]
    h = y * (1.0 + mod_ref[1:2, :]) + mod_ref[0:1, :]
    hb = h.astype(BF16)

    def group(g):
        return _dot(hb, whg_ref[:, g * HG_W:(g + 1) * HG_W])

    qa = group(0)
    q_ref[...] = (qa * _sigmoid(qa)).astype(BF16)
    lbf = lb_ref[0:1, :]
    lff_ref[...] = jnp.log(lbf + (1.0 - lbf) * _sigmoid(group(1)))
    lbb = lb_ref[1:2, :]
    lfb_ref[...] = jnp.log(lbb + (1.0 - lbb) * _sigmoid(group(2)))
    v_ref[...] = group(3).astype(BF16)
    gs_ref[...] = _sigmoid(group(4)).astype(BF16)
    m_ref[...] = _dot(hb, wml_ref[...]).astype(BF16)
    for g in range(2 * D_MODEL // HG_W):
        sl = slice(g * HG_W, (g + 1) * HG_W)
        gt_ref[:, sl] = _sigmoid(_dot(hb, wgt_ref[:, sl])).astype(BF16)


def _mix_in(x, mod, gain, lb, whg, wml, wgt, seq):
    t, d = x.shape
    tm = min(512, seq)
    per_b = seq // tm
    row = lambda i: (i, 0)
    outs = [
        ((t, HG_W), BF16), ((t, HG_W), F32), ((t, HG_W), F32), ((t, HG_W), BF16),
        ((t, HG_W), BF16), ((t, MLA_IN_W), BF16), ((t, 2 * D_MODEL), BF16),
    ]
    return pl.pallas_call(
        _mix_in_kernel,
        grid=(t // tm,),
        in_specs=[
            pl.BlockSpec((tm, d), row),
            pl.BlockSpec((None, 6, d), lambda i: (i // per_b, 0, 0)),
            _const_spec((1, d)),
            _const_spec((2, HG_W)),
            _const_spec(whg.shape),
            _const_spec(wml.shape),
            _const_spec(wgt.shape),
        ],
        out_specs=[pl.BlockSpec((tm, s[1]), row) for s, _ in outs],
        out_shape=[jax.ShapeDtypeStruct(s, dt) for s, dt in outs],
        compiler_params=_params("arbitrary"),
        name="mix_in",
    )(x, mod, gain, lb, whg, wml, wgt)


_HG_LEVELS = (32, 16, 8, 4, 2, 1)


def _hgrn_constants():
    c = HG_CHUNK
    r = np.arange(c)

    def seg(lo, hi):
        return ((r >= lo) & (r <= hi)).astype(np.float32)

    zf = np.zeros((8, c, c), np.float32)
    zb = np.zeros((8, c, c), np.float32)
    masks = np.zeros((len(_HG_LEVELS), c, c), np.float32)
    for p in range(c):
        zf[0, p] = seg(0, p)
        zb[0, p] = seg(p, c - 1)
        zf[7, p] = seg(p + 1, c - 1)
        zb[7, p] = seg(0, p - 1)
    for li, m in enumerate(_HG_LEVELS):
        for p in range(c):
            base = (p // (2 * m)) * 2 * m
            later = (p - base) >= m
            last_early = base + m - 1
            first_late = base + m
            zf[1 + li, p] = seg(last_early + 1, p) if later else seg(p + 1, last_early)
            zb[1 + li, p] = seg(first_late, p - 1) if later else seg(p, first_late - 1)
        blk = r // (2 * m)
        late = (r % (2 * m)) >= m
        masks[li] = ((blk[:, None] == blk[None, :]) & late[:, None] & (~late)[None, :]).astype(np.float32)
    masks_all = np.concatenate([masks, masks.transpose(0, 2, 1)], axis=0)
    return zf.reshape(8 * c, c), zb.reshape(8 * c, c), masks_all


def _hgrn_chunk(lf, qf, vb, zl, mk_ref, mask_base, state_t, total_row):
    c = HG_CHUNK
    hi, lo = _split_bf16(lf)
    sums = _dot(zl, hi) + _dot(zl, lo)
    e = jnp.exp(sums)
    k = 1.0 - jnp.exp(lf)
    scores = jnp.zeros((c, c), F32)
    for li in range(len(_HG_LEVELS)):
        el = e[(li + 1) * c:(li + 2) * c]
        p = _dot_nt((qf * el).astype(BF16), (k * el).astype(BF16))
        scores = scores + jnp.where(mk_ref[mask_base + li] > 0.0, p, 0.0)
    diag = jnp.sum(qf * k, axis=1, keepdims=True)
    q_in = (qf * e[0:c]).astype(BF16)
    o = (_dot(scores.astype(BF16), vb) + _dot_nt(q_in, state_t.astype(BF16))
         + diag * vb.astype(F32))
    k_out = (k * e[7 * c:8 * c]).astype(BF16)
    new_state_t = e[total_row:total_row + 1] * state_t + _dot_tn(vb, k_out)
    return o, new_state_t


def _hgrn_kernel(q_ref, lff_ref, lfb_ref, v_ref, gs_ref, gain_ref, zf_ref, zb_ref, mk_ref,
                 o_ref, of_sc, ob_sc):
    c = HG_CHUNK
    n_chunks = q_ref.shape[0] // c
    n_lv = len(_HG_LEVELS)

    def body(n, carry):
        sf, sb = carry
        rf = pl.multiple_of(n * c, c)
        rb = pl.multiple_of((n_chunks - 1 - n) * c, c)
        o_f, sf = _hgrn_chunk(lff_ref[pl.ds(rf, c), :], q_ref[pl.ds(rf, c), :].astype(F32),
                              v_ref[pl.ds(rf, c), :], zf_ref[...], mk_ref, 0, sf, c - 1)
        of_sc[pl.ds(rf, c), :] = o_f
        o_b, sb = _hgrn_chunk(lfb_ref[pl.ds(rb, c), :], q_ref[pl.ds(rb, c), :].astype(F32),
                              v_ref[pl.ds(rb, c), :], zb_ref[...], mk_ref, n_lv, sb, 0)
        ob_sc[pl.ds(rb, c), :] = o_b
        return sf, sb

    zero = jnp.zeros((HG_DK, HG_DK), F32)
    lax.fori_loop(0, n_chunks, body, (zero, zero))
    o = of_sc[...] + ob_sc[...]
    ms = jnp.mean(o * o, axis=-1, keepdims=True)
    y = o * lax.rsqrt(ms + NORM_EPS) * gain_ref[...]
    o_ref[...] = (y * gs_ref[...].astype(F32)).astype(o_ref.dtype)


def _hgrn(q, lff, lfb, v, gs, gain, batch, seq):
    zf, zb, masks = _hgrn_constants()
    t = q.shape[0]
    blk = pl.BlockSpec((seq, HG_DK), lambda b, h: (b, h))
    return pl.pallas_call(
        _hgrn_kernel,
        grid=(batch, HG_HEADS),
        in_specs=[blk, blk, blk, blk, blk, _const_spec((1, HG_DK)),
                  _const_spec(zf.shape), _const_spec(zb.shape), _const_spec(masks.shape)],
        out_specs=blk,
        out_shape=jax.ShapeDtypeStruct((t, HG_W), BF16),
        scratch_shapes=[pltpu.VMEM((seq, HG_DK), F32), pltpu.VMEM((seq, HG_DK), F32)],
        compiler_params=_params("arbitrary", "arbitrary"),
        name="hgrn",
    )(q, lff, lfb, v, gs, gain, jnp.asarray(zf, BF16), jnp.asarray(zb, BF16), jnp.asarray(masks))


def _mla_prep_kernel(m_ref, cos_ref, sin_ref, gq_ref, gkv_ref, wq1_ref, wq2_ref, wk_ref, wv_ref,
                     e1_ref, e2_ref, q_ref, k_ref, v_ref):
    m = m_ref[...]
    cq = m[:, 0:Q_RANK].astype(F32)
    nq = (cq * lax.rsqrt(jnp.mean(cq * cq, axis=-1, keepdims=True) + NORM_EPS) * gq_ref[...]).astype(BF16)
    ckv = m[:, Q_RANK:Q_RANK + KV_RANK].astype(F32)
    nkv = (ckv * lax.rsqrt(jnp.mean(ckv * ckv, axis=-1, keepdims=True) + NORM_EPS) * gkv_ref[...]).astype(BF16)
    kpe = m[:, Q_RANK + KV_RANK:MLA_IN_W]
    cos = jnp.concatenate([cos_ref[...]] * MLA_HEADS, axis=1)
    sin = jnp.concatenate([sin_ref[...]] * MLA_HEADS, axis=1)
    scale = (NOPE + ROPE) ** -0.5
    q = _dot(nq, wq1_ref[...]) * cos + _dot(nq, wq2_ref[...]) * sin
    q_ref[...] = (q * scale).astype(BF16)
    k = _dot(nkv, wk_ref[...]) + _dot(kpe, e1_ref[...]) * cos + _dot(kpe, e2_ref[...]) * sin
    k_ref[...] = k.astype(BF16)
    v_ref[...] = _dot(nkv, wv_ref[...]).astype(BF16)


def _mla_prep(m, cos_t, sin_t, gq, gkv, wq1, wq2, wk, wv, e1, e2, seq):
    t = m.shape[0]
    tm = min(512, seq)
    row = lambda i: (i, 0)
    qk_w = MLA_HEADS * HEAD_LANES
    return pl.pallas_call(
        _mla_prep_kernel,
        grid=(t // tm,),
        in_specs=[
            pl.BlockSpec((tm, MLA_IN_W), row),
            pl.BlockSpec((tm, HEAD_LANES), row),
            pl.BlockSpec((tm, HEAD_LANES), row),
            _const_spec((1, Q_RANK)), _const_spec((1, KV_RANK)),
            _const_spec(wq1.shape), _const_spec(wq2.shape), _const_spec(wk.shape),
            _const_spec(wv.shape), _const_spec(e1.shape), _const_spec(e2.shape),
        ],
        out_specs=[pl.BlockSpec((tm, qk_w), row), pl.BlockSpec((tm, qk_w), row),
                   pl.BlockSpec((tm, MLA_VW), row)],
        out_shape=[jax.ShapeDtypeStruct((t, qk_w), BF16), jax.ShapeDtypeStruct((t, qk_w), BF16),
                   jax.ShapeDtypeStruct((t, MLA_VW), BF16)],
        compiler_params=_params("arbitrary"),
        name="mla_prep",
    )(m, cos_t, sin_t, gq, gkv, wq1, wq2, wk, wv, e1, e2)


def _mla_attn_kernel(q_ref, k_ref, v_ref, o_ref):
    outs = []
    for h in range(MLA_HEADS):
        qh = q_ref[:, h * HEAD_LANES:(h + 1) * HEAD_LANES]
        kh = k_ref[:, h * HEAD_LANES:(h + 1) * HEAD_LANES]
        s = _dot_nt(qh, kh)
        p = jnp.exp(s - jnp.max(s, axis=-1, keepdims=True))
        denom = jnp.sum(p, axis=-1, keepdims=True)
        vh = v_ref[:, h * MLA_DV:(h + 1) * MLA_DV]
        outs.append(_dot(p.astype(BF16), vh) / denom)
    o_ref[...] = jnp.concatenate(outs, axis=1).astype(o_ref.dtype)


def _mla_attn(q, k, v, batch, seq):
    t = q.shape[0]
    tq = min(256, seq)
    per_b = seq // tq
    qk_w = MLA_HEADS * HEAD_LANES
    return pl.pallas_call(
        _mla_attn_kernel,
        grid=(batch, per_b),
        in_specs=[
            pl.BlockSpec((tq, qk_w), lambda b, i: (b * per_b + i, 0)),
            pl.BlockSpec((seq, qk_w), lambda b, i: (b, 0)),
            pl.BlockSpec((seq, MLA_VW), lambda b, i: (b, 0)),
        ],
        out_specs=pl.BlockSpec((tq, MLA_VW), lambda b, i: (b * per_b + i, 0)),
        out_shape=jax.ShapeDtypeStruct((t, MLA_VW), BF16),
        compiler_params=_params("arbitrary", "arbitrary"),
        name="mla_attn",
    )(q, k, v)


def _mix_out_kernel(ho_ref, mo_ref, gt_ref, x_ref, mod_ref, woh_ref, wom_ref, wo_ref, o_ref):
    ya = _dot(ho_ref[...], woh_ref[...])
    yb = _dot(mo_ref[...], wom_ref[...])
    y = gt_ref[:, 0:D_MODEL].astype(F32) * ya + gt_ref[:, D_MODEL:2 * D_MODEL].astype(F32) * yb
    o_ref[...] = x_ref[...] + mod_ref[2:3, :] * _dot(y.astype(BF16), wo_ref[...])


def _mix_out(ho, mo, gt, x, mod, woh, wom, wo, seq):
    t, d = x.shape
    tm = min(512, seq)
    per_b = seq // tm
    row = lambda i: (i, 0)
    return pl.pallas_call(
        _mix_out_kernel,
        grid=(t // tm,),
        in_specs=[
            pl.BlockSpec((tm, HG_W), row), pl.BlockSpec((tm, MLA_VW), row),
            pl.BlockSpec((tm, 2 * d), row), pl.BlockSpec((tm, d), row),
            pl.BlockSpec((None, 6, d), lambda i: (i // per_b, 0, 0)),
            _const_spec(woh.shape), _const_spec(wom.shape), _const_spec(wo.shape),
        ],
        out_specs=pl.BlockSpec((tm, d), row),
        out_shape=jax.ShapeDtypeStruct((t, d), F32),
        compiler_params=_params("arbitrary"),
        name="mix_out",
    )(ho, mo, gt, x, mod, woh, wom, wo)


def _moe_num_blocks(tokens):
    return -(-(tokens * TOP_K + N_EXPERTS * (MOE_BLOCK - 1)) // MOE_BLOCK)


def _ffn_prep_kernel(x_ref, mod_ref, gain_ref, wrh_ref, wrl_ref, br_ref, tri_ref, upper_ref,
                     h_ref, idx_ref, w_ref, rank_ref, meta_ref, be_ref, count_sc):
    i = pl.program_id(0)
    tm = x_ref.shape[0]

    @pl.when(i == 0)
    def _():
        count_sc[...] = jnp.zeros_like(count_sc)

    x = x_ref[...]
    ms = jnp.mean(x * x, axis=-1, keepdims=True)
    y = x * lax.rsqrt(ms + NORM_EPS) * gain_ref[...]
    h = y * (1.0 + mod_ref[4:5, :]) + mod_ref[3:4, :]
    h_ref[...] = h

    h_hi, h_lo = _split_bf16(h)
    logits = (_dot(h_hi, wrh_ref[...]) + _dot(h_hi, wrl_ref[...]) + _dot(h_lo, wrh_ref[...])
              + br_ref[...])
    lane = lax.broadcasted_iota(jnp.int32, (tm, LANES), 1)
    neg = jnp.float32(-jnp.inf)
    cur = jnp.where(lane < N_EXPERTS, logits, neg)
    vals, ids = [], []
    for _ in range(TOP_K):
        mx = jnp.max(cur, axis=-1, keepdims=True)
        sel = jnp.min(jnp.where(cur == mx, lane, LANES), axis=-1, keepdims=True)
        vals.append(mx)
        ids.append(sel)
        cur = jnp.where(lane == sel, neg, cur)
    ex = [jnp.exp(v - vals[0]) for v in vals]
    tot = ex[0] + ex[1] + ex[2] + ex[3]

    onehot = jnp.zeros((tm, LANES), F32)
    for sel in ids:
        onehot = onehot + jnp.where(lane == sel, 1.0, 0.0)
    before = _dot(tri_ref[...], onehot.astype(BF16)) + count_sc[0:1, :]
    idx_out = jnp.zeros((tm, LANES), jnp.int32)
    w_out = jnp.zeros((tm, LANES), F32)
    rank_out = jnp.zeros((tm, LANES), jnp.int32)
    for kk in range(TOP_K):
        rk = jnp.sum(jnp.where(lane == ids[kk], before, 0.0), axis=-1, keepdims=True)
        idx_out = jnp.where(lane == kk, ids[kk], idx_out)
        w_out = jnp.where(lane == kk, ex[kk] / tot, w_out)
        rank_out = jnp.where(lane == kk, rk.astype(jnp.int32), rank_out)
    idx_ref[...] = idx_out[:, 0:TOP_K]
    w_ref[...] = w_out[:, 0:TOP_K]
    rank_ref[...] = rank_out[:, 0:TOP_K]
    count_sc[...] = count_sc[...] + jnp.sum(onehot, axis=0, keepdims=True)

    @pl.when(i == pl.num_programs(0) - 1)
    def _():
        counts = count_sc[...]
        blocks = jnp.floor((counts + (MOE_BLOCK - 1.0)) / MOE_BLOCK)
        b_hi = jnp.floor(blocks / 16.0)
        b_lo = blocks - 16.0 * b_hi
        up = upper_ref[...]
        rows16 = lambda a: jnp.concatenate([a, a], axis=0).astype(BF16)
        end_blocks = 16.0 * _dot(rows16(b_hi), up) + _dot(rows16(b_lo), up)
        pad_end = end_blocks[0:8] * MOE_BLOCK
        pad_start = pad_end - blocks * MOE_BLOCK
        row8 = lax.broadcasted_iota(jnp.int32, (8, LANES), 0)
        meta = jnp.where(row8 == 0, pad_start, jnp.where(row8 == 1, pad_end, counts))
        meta_ref[...] = meta.astype(jnp.int32)
        nb = be_ref.shape[0]
        starts = (lax.broadcasted_iota(jnp.int32, (nb, LANES), 0) * MOE_BLOCK).astype(F32)
        lane_b = lax.broadcasted_iota(jnp.int32, (nb, LANES), 1)
        passed = jnp.where((pad_end[0:1, :] <= starts) & (lane_b < N_EXPERTS), 1.0, 0.0)
        e_of_block = jnp.minimum(jnp.sum(passed, axis=-1, keepdims=True), N_EXPERTS - 1.0)
        be_ref[...] = jnp.broadcast_to(e_of_block, (nb, LANES)).astype(jnp.int32)


def _ffn_prep(x, mod, gain, wr_hi, wr_lo, br, seq):
    t, d = x.shape
    tm = min(512, seq)
    per_b = seq // tm
    nb = _moe_num_blocks(t)
    nb_pad = -(-nb // 8) * 8
    tri = jnp.asarray(np.tril(np.ones((tm, tm), np.float32), -1), BF16)
    upper = jnp.asarray(np.triu(np.ones((LANES, LANES), np.float32)), BF16)
    row = lambda i: (i, 0)
    return pl.pallas_call(
        _ffn_prep_kernel,
        grid=(t // tm,),
        in_specs=[
            pl.BlockSpec((tm, d), row),
            pl.BlockSpec((None, 6, d), lambda i: (i // per_b, 0, 0)),
            _const_spec((1, d)), _const_spec(wr_hi.shape), _const_spec(wr_lo.shape),
            _const_spec((1, LANES)), _const_spec((tm, tm)), _const_spec((LANES, LANES)),
        ],
        out_specs=[
            pl.BlockSpec((tm, d), row), pl.BlockSpec((tm, TOP_K), row),
            pl.BlockSpec((tm, TOP_K), row), pl.BlockSpec((tm, TOP_K), row),
            _const_spec((8, LANES)), _const_spec((nb_pad, LANES)),
        ],
        out_shape=[
            jax.ShapeDtypeStruct((t, d), F32), jax.ShapeDtypeStruct((t, TOP_K), jnp.int32),
            jax.ShapeDtypeStruct((t, TOP_K), F32), jax.ShapeDtypeStruct((t, TOP_K), jnp.int32),
            jax.ShapeDtypeStruct((8, LANES), jnp.int32), jax.ShapeDtypeStruct((nb_pad, LANES), jnp.int32),
        ],
        scratch_shapes=[pltpu.VMEM((8, LANES), F32)],
        compiler_params=_params("arbitrary"),
        name="ffn_prep",
    )(x, mod, gain, wr_hi, wr_lo, br, tri, upper)


DMA_UNROLL = 4


def _row_copy(src, src_row, dst, dst_row, sem):
    return pltpu.make_async_copy(src.at[pl.ds(src_row, 1)], dst.at[pl.ds(dst_row, 1)], sem)


def _dispatch_kernel(idx_ref, rank_ref, start_ref, h_ref, rows_in, rows_out, sem):
    del rows_in
    tm = h_ref.shape[0]

    def issue(r, carry):
        for kk in range(TOP_K):
            a = r * TOP_K + kk
            dst = start_ref[idx_ref[a]] + rank_ref[a]
            _row_copy(h_ref, r, rows_out, dst, sem).start()
        return carry

    lax.fori_loop(0, tm, issue, 0, unroll=DMA_UNROLL)

    def drain(r, carry):
        for _ in range(TOP_K):
            _row_copy(h_ref, 0, rows_out, 0, sem).wait()
        return carry

    lax.fori_loop(0, tm, drain, 0, unroll=DMA_UNROLL)


def _dispatch(idx_flat, rank_flat, pad_start, h, n_rows):
    t, d = h.shape
    tm = min(256, t)
    na = tm * TOP_K
    smem = functools.partial(pl.BlockSpec, memory_space=pltpu.SMEM)
    return pl.pallas_call(
        _dispatch_kernel,
        grid=(t // tm,),
        in_specs=[
            smem((na,), lambda i: (i,)), smem((na,), lambda i: (i,)),
            smem((N_EXPERTS,), lambda i: (0,)),
            pl.BlockSpec((tm, d), lambda i: (i, 0)),
            pl.BlockSpec(memory_space=pl.ANY),
        ],
        out_specs=pl.BlockSpec(memory_space=pl.ANY),
        out_shape=jax.ShapeDtypeStruct((n_rows, d), F32),
        scratch_shapes=[pltpu.SemaphoreType.DMA(())],
        input_output_aliases={4: 0},
        compiler_params=_params("arbitrary"),
        name="moe_dispatch",
    )(idx_flat, rank_flat, pad_start, h, jnp.zeros((n_rows, d), F32))


PERM_W = 256


def _deinterleave_kernel(w_ref, p_ref, g_ref, u_ref):
    half = PERM_W // 2
    for j in range(w_ref.shape[1] // PERM_W):
        r = _dot(w_ref[:, j * PERM_W:(j + 1) * PERM_W].astype(BF16), p_ref[...])
        g_ref[:, j * half:(j + 1) * half] = r[:, 0:half].astype(BF16)
        u_ref[:, j * half:(j + 1) * half] = r[:, half:PERM_W].astype(BF16)


def _deinterleave_gate_up(w_gate_up):
    depth, n_e, d, two_f = w_gate_up.shape
    half = PERM_W // 2
    perm = np.zeros((PERM_W, PERM_W), np.float32)
    for col in range(half):
        perm[2 * col, col] = 1.0
        perm[2 * col + 1, half + col] = 1.0
    out = jax.ShapeDtypeStruct((depth * n_e, d, two_f // 2), BF16)
    wg, wu = pl.pallas_call(
        _deinterleave_kernel,
        grid=(depth * n_e,),
        in_specs=[pl.BlockSpec((None, d, two_f), lambda i: (i, 0, 0)), _const_spec((PERM_W, PERM_W))],
        out_specs=[pl.BlockSpec((None, d, two_f // 2), lambda i: (i, 0, 0))] * 2,
        out_shape=[out, out],
        compiler_params=_params("arbitrary"),
        name="deinterleave_gate_up",
    )(w_gate_up.reshape(depth * n_e, d, two_f), jnp.asarray(perm, BF16))
    shape = (depth, n_e, d, two_f // 2)
    return wg.reshape(shape), wu.reshape(shape)


def _experts_kernel(be_ref, nact_ref, x_ref, wg_ref, wu_ref, bg_ref, bu_ref, wdn_ref, bdn_ref, o_ref):
    del be_ref
    active = pl.program_id(0) < nact_ref[0]

    @pl.when(jnp.logical_not(active))
    def _():
        o_ref[...] = jnp.zeros_like(o_ref)

    @pl.when(active)
    def _():
        xb = x_ref[...].astype(BF16)
        gate = jnp.minimum(_dot(xb, wg_ref[...]) + bg_ref[...], SWIGLU_LIMIT)
        up = jnp.clip(_dot(xb, wu_ref[...]) + bu_ref[...], -SWIGLU_LIMIT, SWIGLU_LIMIT)
        act = (up + 1.0) * gate * _sigmoid(SWIGLU_ALPHA * gate)
        o_ref[...] = _dot(act.astype(BF16), wdn_ref[...]) + bdn_ref[...]


def _experts(block_e, n_active, x_rows, wg, wu, bg, bu, wdn, bdn, layer):
    n_rows, d = x_rows.shape
    nb = n_rows // MOE_BLOCK

    def row_map(i, be, na):
        return (jnp.minimum(i, na[0] - 1), 0)

    def w_map(i, be, na):
        return (layer, be[i], 0, 0)

    grid_spec = pltpu.PrefetchScalarGridSpec(
        num_scalar_prefetch=2,
        grid=(nb,),
        in_specs=[
            pl.BlockSpec((MOE_BLOCK, d), row_map),
            pl.BlockSpec((None, None, d, D_EXPERT), w_map),
            pl.BlockSpec((None, None, d, D_EXPERT), w_map),
            pl.BlockSpec((None, None, 1, D_EXPERT), w_map),
            pl.BlockSpec((None, None, 1, D_EXPERT), w_map),
            pl.BlockSpec((None, None, D_EXPERT, d), w_map),
            pl.BlockSpec((None, None, 1, d), w_map),
        ],
        out_specs=pl.BlockSpec((MOE_BLOCK, d), lambda i, be, na: (i, 0)),
    )
    return pl.pallas_call(
        _experts_kernel,
        grid_spec=grid_spec,
        out_shape=jax.ShapeDtypeStruct((n_rows, d), F32),
        compiler_params=_params("arbitrary"),
        name="moe_experts",
    )(block_e, n_active, x_rows, wg, wu, bg, bu, wdn, bdn)


def _combine_kernel(idx_ref, rank_ref, start_ref, y_hbm, w_ref, x_ref, mod_ref, fin_ref,
                    o_ref, buf, sem, *, final_norm):
    tm = x_ref.shape[0]

    def issue(r, carry):
        for kk in range(TOP_K):
            a = r * TOP_K + kk
            src = start_ref[idx_ref[a]] + rank_ref[a]
            _row_copy(y_hbm, src, buf.at[kk], r, sem).start()
        return carry

    lax.fori_loop(0, tm, issue, 0, unroll=DMA_UNROLL)

    def drain(r, carry):
        for _ in range(TOP_K):
            _row_copy(y_hbm, 0, buf.at[0], 0, sem).wait()
        return carry

    lax.fori_loop(0, tm, drain, 0, unroll=DMA_UNROLL)

    w = w_ref[...]
    moe = w[:, 0:1] * buf[0]
    for kk in range(1, TOP_K):
        moe = moe + w[:, kk:kk + 1] * buf[kk]
    out = x_ref[...] + mod_ref[5:6, :] * moe
    if final_norm:
        ms = jnp.mean(out * out, axis=-1, keepdims=True)
        out = out * lax.rsqrt(ms + NORM_EPS) * fin_ref[...]
    o_ref[...] = out


def _combine(idx_flat, rank_flat, pad_start, y_rows, w, x, mod, fin_gain, seq, final_norm):
    t, d = x.shape
    tm = min(256, seq)
    per_b = seq // tm
    na = tm * TOP_K
    smem = functools.partial(pl.BlockSpec, memory_space=pltpu.SMEM)
    row = lambda i: (i, 0)
    return pl.pallas_call(
        functools.partial(_combine_kernel, final_norm=final_norm),
        grid=(t // tm,),
        in_specs=[
            smem((na,), lambda i: (i,)), smem((na,), lambda i: (i,)),
            smem((N_EXPERTS,), lambda i: (0,)),
            pl.BlockSpec(memory_space=pl.ANY),
            pl.BlockSpec((tm, TOP_K), row), pl.BlockSpec((tm, d), row),
            pl.BlockSpec((None, 6, d), lambda i: (i // per_b, 0, 0)),
            _const_spec((1, d)),
        ],
        out_specs=pl.BlockSpec((tm, d), row),
        out_shape=jax.ShapeDtypeStruct((t, d), F32),
        scratch_shapes=[pltpu.VMEM((TOP_K, tm, d), F32), pltpu.SemaphoreType.DMA(())],
        compiler_params=_params("arbitrary"),
        name="moe_combine",
    )(idx_flat, rank_flat, pad_start, y_rows, w, x, mod, fin_gain)


def _rope_tables(positions):
    inv_freq = ROPE_THETA ** (-jnp.arange(0, ROPE, 2, dtype=F32) / ROPE)
    ang = positions.astype(F32).reshape(-1)[:, None] * inv_freq
    cos, sin = jnp.cos(ang), jnp.sin(ang)
    t = cos.shape[0]
    pad = jnp.zeros((t, HEAD_LANES - NOPE - ROPE), F32)
    cos_t = jnp.concatenate([jnp.ones((t, NOPE), F32), cos, cos, pad], axis=1)
    sin_t = jnp.concatenate([jnp.zeros((t, NOPE), F32), sin, sin, pad], axis=1)
    return cos_t, sin_t


def _rotate_half_matrix():
    half = ROPE // 2
    r = np.zeros((ROPE, ROPE), np.float32)
    for i in range(half):
        r[half + i, i] = -1.0
        r[i, half + i] = 1.0
    return r


def _mla_weights(w_q_up, w_kv_up):
    rot = jnp.asarray(_rotate_half_matrix())
    wq = w_q_up.reshape(Q_RANK, MLA_HEADS, NOPE + ROPE)
    zq = jnp.zeros((Q_RANK, MLA_HEADS, HEAD_LANES - NOPE - ROPE), F32)
    wq1 = jnp.concatenate([wq, zq], axis=2)
    wq_rot = jnp.einsum("rhd,de->rhe", wq[:, :, NOPE:], rot)
    wq2 = jnp.concatenate([jnp.zeros((Q_RANK, MLA_HEADS, NOPE), F32), wq_rot, zq], axis=2)
    wkv = w_kv_up.reshape(KV_RANK, MLA_HEADS, NOPE + MLA_DV)
    wk = jnp.concatenate([wkv[:, :, :NOPE], jnp.zeros((KV_RANK, MLA_HEADS, HEAD_LANES - NOPE), F32)], axis=2)
    wv = wkv[:, :, NOPE:]
    qk_w = MLA_HEADS * HEAD_LANES
    return (wq1.reshape(Q_RANK, qk_w).astype(BF16), wq2.reshape(Q_RANK, qk_w).astype(BF16),
            wk.reshape(KV_RANK, qk_w).astype(BF16), wv.reshape(KV_RANK, MLA_VW).astype(BF16))


def _rope_placement():
    rot = _rotate_half_matrix()
    e1 = np.zeros((LANES, MLA_HEADS, HEAD_LANES), np.float32)
    e2 = np.zeros((LANES, MLA_HEADS, HEAD_LANES), np.float32)
    for h in range(MLA_HEADS):
        e1[:ROPE, h, NOPE:NOPE + ROPE] = np.eye(ROPE, dtype=np.float32)
        e2[:ROPE, h, NOPE:NOPE + ROPE] = rot
    shape = (LANES, MLA_HEADS * HEAD_LANES)
    return jnp.asarray(e1.reshape(shape), BF16), jnp.asarray(e2.reshape(shape), BF16)


def _in_proj_weights(w_in_l):
    hg = w_in_l[:, 0:5 * HG_W]
    o = 5 * HG_W
    ml = w_in_l[:, o:o + Q_RANK + KV_RANK + ROPE]
    ml = jnp.concatenate([ml, jnp.zeros((D_MODEL, MLA_IN_W - ml.shape[1]), F32)], axis=1)
    gt = w_in_l[:, o + Q_RANK + KV_RANK + ROPE:]
    return hg.astype(BF16), ml.astype(BF16), gt.astype(BF16)


def kernel(x, c, positions, w_ada, b_ada, norm_mix_gain, norm_ffn_gain, w_in, hgrn_lb_logits,
           hgrn_out_norm_gain, w_out_hgrn, mla_q_norm_gain, w_q_up, mla_kv_norm_gain, w_kv_up,
           w_out_mla, w_o, w_router, b_router, w_gate_up, b_gate_up, w_down, b_down,
           final_norm_gain):
    batch, seq, d = x.shape
    depth = w_ada.shape[0]
    t = batch * seq
    n_rows = _moe_num_blocks(t) * MOE_BLOCK

    mod_all = _ada_modulation(c, w_ada, b_ada).reshape(depth, batch, 6, d)
    lb_all = _hgrn_lower_bounds(hgrn_lb_logits)
    cos_t, sin_t = _rope_tables(positions)
    e1, e2 = _rope_placement()

    wg, wu = _deinterleave_gate_up(w_gate_up)
    bg = b_gate_up[..., 0::2][:, :, None, :]
    bu = b_gate_up[..., 1::2][:, :, None, :]
    wdn = w_down.astype(BF16)
    bdn = b_down[:, :, None, :]
    wr_pad = jnp.pad(w_router, ((0, 0), (0, 0), (0, LANES - N_EXPERTS)))
    br_pad = jnp.pad(b_router, ((0, 0), (0, LANES - N_EXPERTS)))

    xs = x.reshape(t, d)
    for l in range(depth):
        mod = mod_all[l]
        whg, wml, wgt = _in_proj_weights(w_in[l])
        q, lff, lfb, v, gs, m, gt = _mix_in(xs, mod, norm_mix_gain[l][None, :], lb_all[l], whg, wml, wgt, seq)
        ho = _hgrn(q, lff, lfb, v, gs, hgrn_out_norm_gain[l][None, :], batch, seq)
        wq1, wq2, wk, wv = _mla_weights(w_q_up[l], w_kv_up[l])
        qh, kh, vh = _mla_prep(m, cos_t, sin_t, mla_q_norm_gain[l][None, :], mla_kv_norm_gain[l][None, :],
                               wq1, wq2, wk, wv, e1, e2, seq)
        mo = _mla_attn(qh, kh, vh, batch, seq)
        xs = _mix_out(ho, mo, gt, xs, mod, w_out_hgrn[l].astype(BF16), w_out_mla[l].astype(BF16),
                      w_o[l].astype(BF16), seq)

        wr_hi = wr_pad[l].astype(BF16)
        wr_lo = (wr_pad[l] - wr_hi.astype(F32)).astype(BF16)
        h2, idx, tw, rank, meta, block_e = _ffn_prep(xs, mod, norm_ffn_gain[l][None, :], wr_hi, wr_lo,
                                                     br_pad[l][None, :], seq)
        pad_start = meta[0, :N_EXPERTS]
        n_active = meta[1, N_EXPERTS - 1:N_EXPERTS] // MOE_BLOCK
        idx_flat = idx.reshape(-1)
        rank_flat = rank.reshape(-1)
        x_rows = _dispatch(idx_flat, rank_flat, pad_start, h2, n_rows)
        y_rows = _experts(block_e[:n_rows // MOE_BLOCK, 0], n_active, x_rows, wg, wu, bg, bu, wdn, bdn, l)
        xs = _combine(idx_flat, rank_flat, pad_start, y_rows, tw, xs, mod, final_norm_gain[None, :],
                      seq, final_norm=(l == depth - 1))
    return xs.reshape(batch, seq, d)
```

```python
import functools

import numpy as np
import jax
import jax.numpy as jnp
from jax import lax
from jax.experimental import pallas as pl
from jax.experimental.pallas import tpu as pltpu

F32 = jnp.float32
BF16 = jnp.bfloat16

D_MODEL = 1024
HG_HEADS = 4
HG_DK = 128
HG_W = HG_HEADS * HG_DK
HG_CHUNK = 64
MLA_HEADS = 8
Q_RANK = 384
KV_RANK = 256
NOPE = 64
ROPE = 32
MLA_DV = 64
MLA_VW = MLA_HEADS * MLA_DV
MLA_IN_W = 768
HEAD_LANES = 128
ROPE_THETA = 10000.0
N_EXPERTS = 32
TOP_K = 4
D_EXPERT = 1024
MOE_BLOCK = 256
SWIGLU_LIMIT = 7.0
SWIGLU_ALPHA = 1.702
NORM_EPS = 1e-6
LANES = 128
VMEM_LIMIT = 56 * 1024 * 1024

_NT = (((1,), (1,)), ((), ()))
_TN = (((0,), (0,)), ((), ()))


def _dot(a, b):
    return jnp.dot(a, b, preferred_element_type=F32)


def _dot_nt(a, b):
    return lax.dot_general(a, b, _NT, preferred_element_type=F32)


def _dot_tn(a, b):
    return lax.dot_general(a, b, _TN, preferred_element_type=F32)


def _sigmoid(x):
    return 1.0 / (1.0 + jnp.exp(-x))


def _split_bf16(x):
    hi = x.astype(BF16)
    lo = (x - hi.astype(F32)).astype(BF16)
    return hi, lo


def _params(*sem):
    return pltpu.CompilerParams(dimension_semantics=sem, vmem_limit_bytes=VMEM_LIMIT)


def _const_spec(shape):
    nd = len(shape)
    return pl.BlockSpec(shape, lambda *_: (0,) * nd)


def _ada_kernel(c_ref, w_ref, b_ref, o_ref):
    c = c_ref[...]
    cond = (c * _sigmoid(c)).astype(BF16)
    o_ref[...] = _dot(cond, w_ref[...].astype(BF16)) + b_ref[...]


def _ada_modulation(c, w_ada, b_ada):
    depth, d, n = w_ada.shape
    b = c.shape[0]
    tn = 1536
    return pl.pallas_call(
        _ada_kernel,
        grid=(depth, n // tn),
        in_specs=[
            pl.BlockSpec((b, d), lambda l, j: (0, 0)),
            pl.BlockSpec((None, d, tn), lambda l, j: (l, 0, j)),
            pl.BlockSpec((None, 1, tn), lambda l, j: (l, 0, j)),
        ],
        out_specs=pl.BlockSpec((None, b, tn), lambda l, j: (l, 0, j)),
        out_shape=jax.ShapeDtypeStruct((depth, b, n), F32),
        compiler_params=_params("arbitrary", "arbitrary"),
        name="ada_modulation",
    )(c, w_ada, b_ada.reshape(depth, 1, n))


def _lb_kernel(x_ref, o_ref):
    depth = x_ref.shape[0]
    x = [x_ref[l] for l in range(depth)]
    m = x[0]
    for l in range(1, depth):
        m = jnp.maximum(m, x[l])
    e = [jnp.exp(v - m) for v in x]
    tot = e[0]
    for l in range(1, depth):
        tot = tot + e[l]
    sm = [v / tot for v in e]
    acc = sm[0]
    o_ref[0] = acc - sm[0]
    for l in range(1, depth):
        acc = acc + sm[l]
        o_ref[l] = acc - sm[0]


def _hgrn_lower_bounds(lb_logits):
    return pl.pallas_call(
        _lb_kernel,
        out_shape=jax.ShapeDtypeStruct(lb_logits.shape, F32),
        name="hgrn_lower_bounds",
    )(lb_logits.astype(F32))


def _mix_in_kernel(x_ref, mod_ref, gain_ref, lb_ref, whg_ref, wml_ref, wgt_ref,
                   q_ref, lff_ref, lfb_ref, v_ref, gs_ref, m_ref, gt_ref):
    x = x_ref[...]
    ms = jnp.mean(x * x, axis=-1, keepdims=True)
    y = x * lax.rsqrt(ms + NORM_EPS) * gain_ref[...]
    h = y * (1.0 + mod_ref[1:2, :]) + mod_ref[0:1, :]
    hb = h.astype(BF16)

    def group(g):
        return _dot(hb, whg_ref[:, g * HG_W:(g + 1) * HG_W])

    qa = group(0)
    q_ref[...] = (qa * _sigmoid(qa)).astype(BF16)
    lbf = lb_ref[0:1, :]
    lff_ref[...] = jnp.log(lbf + (1.0 - lbf) * _sigmoid(group(1)))
    lbb = lb_ref[1:2, :]
    lfb_ref[...] = jnp.log(lbb + (1.0 - lbb) * _sigmoid(group(2)))
    v_ref[...] = group(3).astype(BF16)
    gs_ref[...] = _sigmoid(group(4)).astype(BF16)
    m_ref[...] = _dot(hb, wml_ref[...]).astype(BF16)
    for g in range(2 * D_MODEL // HG_W):
        sl = slice(g * HG_W, (g + 1) * HG_W)
        gt_ref[:, sl] = _sigmoid(_dot(hb, wgt_ref[:, sl])).astype(BF16)


def _mix_in(x, mod, gain, lb, whg, wml, wgt, seq):
    t, d = x.shape
    tm = min(512, seq)
    per_b = seq // tm
    row = lambda i: (i, 0)
    outs = [
        ((t, HG_W), BF16), ((t, HG_W), F32), ((t, HG_W), F32), ((t, HG_W), BF16),
        ((t, HG_W), BF16), ((t, MLA_IN_W), BF16), ((t, 2 * D_MODEL), BF16),
    ]
    return pl.pallas_call(
        _mix_in_kernel,
        grid=(t // tm,),
        in_specs=[
            pl.BlockSpec((tm, d), row),
            pl.BlockSpec((None, 6, d), lambda i: (i // per_b, 0, 0)),
            _const_spec((1, d)),
            _const_spec((2, HG_W)),
            _const_spec(whg.shape),
            _const_spec(wml.shape),
            _const_spec(wgt.shape),
        ],
        out_specs=[pl.BlockSpec((tm, s[1]), row) for s, _ in outs],
        out_shape=[jax.ShapeDtypeStruct(s, dt) for s, dt in outs],
        compiler_params=_params("arbitrary"),
        name="mix_in",
    )(x, mod, gain, lb, whg, wml, wgt)


_HG_LEVELS = (32, 16, 8, 4, 2, 1)


def _hgrn_constants():
    c = HG_CHUNK
    r = np.arange(c)

    def seg(lo, hi):
        return ((r >= lo) & (r <= hi)).astype(np.float32)

    zf = np.zeros((8, c, c), np.float32)
    zb = np.zeros((8, c, c), np.float32)
    masks = np.zeros((len(_HG_LEVELS), c, c), np.float32)
    for p in range(c):
        zf[0, p] = seg(0, p)
        zb[0, p] = seg(p, c - 1)
        zf[7, p] = seg(p + 1, c - 1)
        zb[7, p] = seg(0, p - 1)
    for li, m in enumerate(_HG_LEVELS):
        for p in range(c):
            base = (p // (2 * m)) * 2 * m
            later = (p - base) >= m
            last_early = base + m - 1
            first_late = base + m
            zf[1 + li, p] = seg(last_early + 1, p) if later else seg(p + 1, last_early)
            zb[1 + li, p] = seg(first_late, p - 1) if later else seg(p, first_late - 1)
        blk = r // (2 * m)
        late = (r % (2 * m)) >= m
        masks[li] = ((blk[:, None] == blk[None, :]) & late[:, None] & (~late)[None, :]).astype(np.float32)
    masks_all = np.concatenate([masks, masks.transpose(0, 2, 1)], axis=0)
    return zf.reshape(8 * c, c), zb.reshape(8 * c, c), masks_all


def _hgrn_chunk(lf, qf, vb, zl, mk_ref, mask_base, state_t, total_row):
    c = HG_CHUNK
    hi, lo = _split_bf16(lf)
    sums = _dot(zl, hi) + _dot(zl, lo)
    e = jnp.exp(sums)
    k = 1.0 - jnp.exp(lf)
    scores = jnp.zeros((c, c), F32)
    for li in range(len(_HG_LEVELS)):
        el = e[(li + 1) * c:(li + 2) * c]
        p = _dot_nt((qf * el).astype(BF16), (k * el).astype(BF16))
        scores = scores + jnp.where(mk_ref[mask_base + li] > 0.0, p, 0.0)
    diag = jnp.sum(qf * k, axis=1, keepdims=True)
    q_in = (qf * e[0:c]).astype(BF16)
    o = (_dot(scores.astype(BF16), vb) + _dot_nt(q_in, state_t.astype(BF16))
         + diag * vb.astype(F32))
    k_out = (k * e[7 * c:8 * c]).astype(BF16)
    new_state_t = e[total_row:total_row + 1] * state_t + _dot_tn(vb, k_out)
    return o, new_state_t


def _hgrn_kernel(q_ref, lff_ref, lfb_ref, v_ref, gs_ref, gain_ref, zf_ref, zb_ref, mk_ref,
                 o_ref, of_sc, ob_sc):
    c = HG_CHUNK
    n_chunks = q_ref.shape[0] // c
    n_lv = len(_HG_LEVELS)

    def body(n, carry):
        sf, sb = carry
        rf = pl.multiple_of(n * c, c)
        rb = pl.multiple_of((n_chunks - 1 - n) * c, c)
        o_f, sf = _hgrn_chunk(lff_ref[pl.ds(rf, c), :], q_ref[pl.ds(rf, c), :].astype(F32),
                              v_ref[pl.ds(rf, c), :], zf_ref[...], mk_ref, 0, sf, c - 1)
        of_sc[pl.ds(rf, c), :] = o_f
        o_b, sb = _hgrn_chunk(lfb_ref[pl.ds(rb, c), :], q_ref[pl.ds(rb, c), :].astype(F32),
                              v_ref[pl.ds(rb, c), :], zb_ref[...], mk_ref, n_lv, sb, 0)
        ob_sc[pl.ds(rb, c), :] = o_b
        return sf, sb

    zero = jnp.zeros((HG_DK, HG_DK), F32)
    lax.fori_loop(0, n_chunks, body, (zero, zero))
    o = of_sc[...] + ob_sc[...]
    ms = jnp.mean(o * o, axis=-1, keepdims=True)
    y = o * lax.rsqrt(ms + NORM_EPS) * gain_ref[...]
    o_ref[...] = (y * gs_ref[...].astype(F32)).astype(o_ref.dtype)


def _hgrn(q, lff, lfb, v, gs, gain, batch, seq):
    zf, zb, masks = _hgrn_constants()
    t = q.shape[0]
    blk = pl.BlockSpec((seq, HG_DK), lambda b, h: (b, h))
    return pl.pallas_call(
        _hgrn_kernel,
        grid=(batch, HG_HEADS),
        in_specs=[blk, blk, blk, blk, blk, _const_spec((1, HG_DK)),
                  _const_spec(zf.shape), _const_spec(zb.shape), _const_spec(masks.shape)],
        out_specs=blk,
        out_shape=jax.ShapeDtypeStruct((t, HG_W), BF16),
        scratch_shapes=[pltpu.VMEM((seq, HG_DK), F32), pltpu.VMEM((seq, HG_DK), F32)],
        compiler_params=_params("arbitrary", "arbitrary"),
        name="hgrn",
    )(q, lff, lfb, v, gs, gain, jnp.asarray(zf, BF16), jnp.asarray(zb, BF16), jnp.asarray(masks))


def _mla_prep_kernel(m_ref, cos_ref, sin_ref, gq_ref, gkv_ref, wq1_ref, wq2_ref, wk_ref, wv_ref,
                     e1_ref, e2_ref, q_ref, k_ref, v_ref):
    m = m_ref[...]
    cq = m[:, 0:Q_RANK].astype(F32)
    nq = (cq * lax.rsqrt(jnp.mean(cq * cq, axis=-1, keepdims=True) + NORM_EPS) * gq_ref[...]).astype(BF16)
    ckv = m[:, Q_RANK:Q_RANK + KV_RANK].astype(F32)
    nkv = (ckv * lax.rsqrt(jnp.mean(ckv * ckv, axis=-1, keepdims=True) + NORM_EPS) * gkv_ref[...]).astype(BF16)
    kpe = m[:, Q_RANK + KV_RANK:MLA_IN_W]
    cos = jnp.concatenate([cos_ref[...]] * MLA_HEADS, axis=1)
    sin = jnp.concatenate([sin_ref[...]] * MLA_HEADS, axis=1)
    scale = (NOPE + ROPE) ** -0.5
    q = _dot(nq, wq1_ref[...]) * cos + _dot(nq, wq2_ref[...]) * sin
    q_ref[...] = (q * scale).astype(BF16)
    k = _dot(nkv, wk_ref[...]) + _dot(kpe, e1_ref[...]) * cos + _dot(kpe, e2_ref[...]) * sin
    k_ref[...] = k.astype(BF16)
    v_ref[...] = _dot(nkv, wv_ref[...]).astype(BF16)


def _mla_prep(m, cos_t, sin_t, gq, gkv, wq1, wq2, wk, wv, e1, e2, seq):
    t = m.shape[0]
    tm = min(512, seq)
    row = lambda i: (i, 0)
    qk_w = MLA_HEADS * HEAD_LANES
    return pl.pallas_call(
        _mla_prep_kernel,
        grid=(t // tm,),
        in_specs=[
            pl.BlockSpec((tm, MLA_IN_W), row),
            pl.BlockSpec((tm, HEAD_LANES), row),
            pl.BlockSpec((tm, HEAD_LANES), row),
            _const_spec((1, Q_RANK)), _const_spec((1, KV_RANK)),
            _const_spec(wq1.shape), _const_spec(wq2.shape), _const_spec(wk.shape),
            _const_spec(wv.shape), _const_spec(e1.shape), _const_spec(e2.shape),
        ],
        out_specs=[pl.BlockSpec((tm, qk_w), row), pl.BlockSpec((tm, qk_w), row),
                   pl.BlockSpec((tm, MLA_VW), row)],
        out_shape=[jax.ShapeDtypeStruct((t, qk_w), BF16), jax.ShapeDtypeStruct((t, qk_w), BF16),
                   jax.ShapeDtypeStruct((t, MLA_VW), BF16)],
        compiler_params=_params("arbitrary"),
        name="mla_prep",
    )(m, cos_t, sin_t, gq, gkv, wq1, wq2, wk, wv, e1, e2)


def _mla_attn_kernel(q_ref, k_ref, v_ref, o_ref):
    outs = []
    for h in range(MLA_HEADS):
        qh = q_ref[:, h * HEAD_LANES:(h + 1) * HEAD_LANES]
        kh = k_ref[:, h * HEAD_LANES:(h + 1) * HEAD_LANES]
        s = _dot_nt(qh, kh)
        p = jnp.exp(s - jnp.max(s, axis=-1, keepdims=True))
        denom = jnp.sum(p, axis=-1, keepdims=True)
        vh = v_ref[:, h * MLA_DV:(h + 1) * MLA_DV]
        outs.append(_dot(p.astype(BF16), vh) / denom)
    o_ref[...] = jnp.concatenate(outs, axis=1).astype(o_ref.dtype)


def _mla_attn(q, k, v, batch, seq):
    t = q.shape[0]
    tq = min(512, seq)
    per_b = seq // tq
    qk_w = MLA_HEADS * HEAD_LANES
    return pl.pallas_call(
        _mla_attn_kernel,
        grid=(batch, per_b),
        in_specs=[
            pl.BlockSpec((tq, qk_w), lambda b, i: (b * per_b + i, 0)),
            pl.BlockSpec((seq, qk_w), lambda b, i: (b, 0)),
            pl.BlockSpec((seq, MLA_VW), lambda b, i: (b, 0)),
        ],
        out_specs=pl.BlockSpec((tq, MLA_VW), lambda b, i: (b * per_b + i, 0)),
        out_shape=jax.ShapeDtypeStruct((t, MLA_VW), BF16),
        compiler_params=_params("arbitrary", "arbitrary"),
        name="mla_attn",
    )(q, k, v)


def _mix_out_kernel(ho_ref, mo_ref, gt_ref, x_ref, mod_ref, woh_ref, wom_ref, wo_ref, o_ref):
    ya = _dot(ho_ref[...], woh_ref[...])
    yb = _dot(mo_ref[...], wom_ref[...])
    y = gt_ref[:, 0:D_MODEL].astype(F32) * ya + gt_ref[:, D_MODEL:2 * D_MODEL].astype(F32) * yb
    o_ref[...] = x_ref[...] + mod_ref[2:3, :] * _dot(y.astype(BF16), wo_ref[...])


def _mix_out(ho, mo, gt, x, mod, woh, wom, wo, seq):
    t, d = x.shape
    tm = min(512, seq)
    per_b = seq // tm
    row = lambda i: (i, 0)
    return pl.pallas_call(
        _mix_out_kernel,
        grid=(t // tm,),
        in_specs=[
            pl.BlockSpec((tm, HG_W), row), pl.BlockSpec((tm, MLA_VW), row),
            pl.BlockSpec((tm, 2 * d), row), pl.BlockSpec((tm, d), row),
            pl.BlockSpec((None, 6, d), lambda i: (i // per_b, 0, 0)),
            _const_spec(woh.shape), _const_spec(wom.shape), _const_spec(wo.shape),
        ],
        out_specs=pl.BlockSpec((tm, d), row),
        out_shape=jax.ShapeDtypeStruct((t, d), F32),
        compiler_params=_params("arbitrary"),
        name="mix_out",
    )(ho, mo, gt, x, mod, woh, wom, wo)


def _moe_num_blocks(tokens):
    return -(-(tokens * TOP_K + N_EXPERTS * (MOE_BLOCK - 1)) // MOE_BLOCK)


def _ffn_prep_kernel(x_ref, mod_ref, gain_ref, wrh_ref, wrl_ref, br_ref, tri_ref, upper_ref,
                     h_ref, idx_ref, w_ref, rank_ref, meta_ref, be_ref, count_sc):
    i = pl.program_id(0)
    tm = x_ref.shape[0]

    @pl.when(i == 0)
    def _():
        count_sc[...] = jnp.zeros_like(count_sc)

    x = x_ref[...]
    ms = jnp.mean(x * x, axis=-1, keepdims=True)
    y = x * lax.rsqrt(ms + NORM_EPS) * gain_ref[...]
    h = y * (1.0 + mod_ref[4:5, :]) + mod_ref[3:4, :]
    h_ref[...] = h

    h_hi, h_lo = _split_bf16(h)
    logits = (_dot(h_hi, wrh_ref[...]) + _dot(h_hi, wrl_ref[...]) + _dot(h_lo, wrh_ref[...])
              + br_ref[...])
    lane = lax.broadcasted_iota(jnp.int32, (tm, LANES), 1)
    neg = jnp.float32(-jnp.inf)
    cur = jnp.where(lane < N_EXPERTS, logits, neg)
    vals, ids = [], []
    for _ in range(TOP_K):
        mx = jnp.max(cur, axis=-1, keepdims=True)
        sel = jnp.min(jnp.where(cur == mx, lane, LANES), axis=-1, keepdims=True)
        vals.append(mx)
        ids.append(sel)
        cur = jnp.where(lane == sel, neg, cur)
    ex = [jnp.exp(v - vals[0]) for v in vals]
    tot = ex[0] + ex[1] + ex[2] + ex[3]

    onehot = jnp.zeros((tm, LANES), F32)
    for sel in ids:
        onehot = onehot + jnp.where(lane == sel, 1.0, 0.0)
    before = _dot(tri_ref[...], onehot.astype(BF16)) + count_sc[0:1, :]
    idx_out = jnp.zeros((tm, LANES), jnp.int32)
    w_out = jnp.zeros((tm, LANES), F32)
    rank_out = jnp.zeros((tm, LANES), jnp.int32)
    for kk in range(TOP_K):
        rk = jnp.sum(jnp.where(lane == ids[kk], before, 0.0), axis=-1, keepdims=True)
        idx_out = jnp.where(lane == kk, ids[kk], idx_out)
        w_out = jnp.where(lane == kk, ex[kk] / tot, w_out)
        rank_out = jnp.where(lane == kk, rk.astype(jnp.int32), rank_out)
    idx_ref[...] = idx_out[:, 0:TOP_K]
    w_ref[...] = w_out[:, 0:TOP_K]
    rank_ref[...] = rank_out[:, 0:TOP_K]
    count_sc[...] = count_sc[...] + jnp.sum(onehot, axis=0, keepdims=True)

    @pl.when(i == pl.num_programs(0) - 1)
    def _():
        counts = count_sc[...]
        blocks = jnp.floor((counts + (MOE_BLOCK - 1.0)) / MOE_BLOCK)
        b_hi = jnp.floor(blocks / 16.0)
        b_lo = blocks - 16.0 * b_hi
        up = upper_ref[...]
        rows16 = lambda a: jnp.concatenate([a, a], axis=0).astype(BF16)
        end_blocks = 16.0 * _dot(rows16(b_hi), up) + _dot(rows16(b_lo), up)
        pad_end = end_blocks[0:8] * MOE_BLOCK
        pad_start = pad_end - blocks * MOE_BLOCK
        row8 = lax.broadcasted_iota(jnp.int32, (8, LANES), 0)
        meta = jnp.where(row8 == 0, pad_start, jnp.where(row8 == 1, pad_end, counts))
        meta_ref[...] = meta.astype(jnp.int32)
        nb = be_ref.shape[0]
        starts = (lax.broadcasted_iota(jnp.int32, (nb, LANES), 0) * MOE_BLOCK).astype(F32)
        lane_b = lax.broadcasted_iota(jnp.int32, (nb, LANES), 1)
        passed = jnp.where((pad_end[0:1, :] <= starts) & (lane_b < N_EXPERTS), 1.0, 0.0)
        e_of_block = jnp.minimum(jnp.sum(passed, axis=-1, keepdims=True), N_EXPERTS - 1.0)
        be_ref[...] = jnp.broadcast_to(e_of_block, (nb, LANES)).astype(jnp.int32)


def _ffn_prep(x, mod, gain, wr_hi, wr_lo, br, seq):
    t, d = x.shape
    tm = min(512, seq)
    per_b = seq // tm
    nb = _moe_num_blocks(t)
    nb_pad = -(-nb // 8) * 8
    tri = jnp.asarray(np.tril(np.ones((tm, tm), np.float32), -1), BF16)
    upper = jnp.asarray(np.triu(np.ones((LANES, LANES), np.float32)), BF16)
    row = lambda i: (i, 0)
    return pl.pallas_call(
        _ffn_prep_kernel,
        grid=(t // tm,),
        in_specs=[
            pl.BlockSpec((tm, d), row),
            pl.BlockSpec((None, 6, d), lambda i: (i // per_b, 0, 0)),
            _const_spec((1, d)), _const_spec(wr_hi.shape), _const_spec(wr_lo.shape),
            _const_spec((1, LANES)), _const_spec((tm, tm)), _const_spec((LANES, LANES)),
        ],
        out_specs=[
            pl.BlockSpec((tm, d), row), pl.BlockSpec((tm, TOP_K), row),
            pl.BlockSpec((tm, TOP_K), row), pl.BlockSpec((tm, TOP_K), row),
            _const_spec((8, LANES)), _const_spec((nb_pad, LANES)),
        ],
        out_shape=[
            jax.ShapeDtypeStruct((t, d), F32), jax.ShapeDtypeStruct((t, TOP_K), jnp.int32),
            jax.ShapeDtypeStruct((t, TOP_K), F32), jax.ShapeDtypeStruct((t, TOP_K), jnp.int32),
            jax.ShapeDtypeStruct((8, LANES), jnp.int32), jax.ShapeDtypeStruct((nb_pad, LANES), jnp.int32),
        ],
        scratch_shapes=[pltpu.VMEM((8, LANES), F32)],
        compiler_params=_params("arbitrary"),
        name="ffn_prep",
    )(x, mod, gain, wr_hi, wr_lo, br, tri, upper)


DMA_UNROLL = 4


def _row_copy(src, src_row, dst, dst_row, sem):
    return pltpu.make_async_copy(src.at[pl.ds(src_row, 1)], dst.at[pl.ds(dst_row, 1)], sem)


def _dispatch_kernel(idx_ref, rank_ref, start_ref, h_ref, rows_in, rows_out, sem):
    del rows_in
    tm = h_ref.shape[0]

    def issue(r, carry):
        for kk in range(TOP_K):
            a = r * TOP_K + kk
            dst = start_ref[idx_ref[a]] + rank_ref[a]
            _row_copy(h_ref, r, rows_out, dst, sem).start(priority=kk % 2)
        return carry

    lax.fori_loop(0, tm, issue, 0, unroll=DMA_UNROLL)

    def drain(r, carry):
        for _ in range(TOP_K):
            _row_copy(h_ref, 0, rows_out, 0, sem).wait()
        return carry

    lax.fori_loop(0, tm, drain, 0, unroll=DMA_UNROLL)


def _dispatch(idx_flat, rank_flat, pad_start, h, n_rows):
    t, d = h.shape
    tm = min(256, t)
    na = tm * TOP_K
    smem = functools.partial(pl.BlockSpec, memory_space=pltpu.SMEM)
    return pl.pallas_call(
        _dispatch_kernel,
        grid=(t // tm,),
        in_specs=[
            smem((na,), lambda i: (i,)), smem((na,), lambda i: (i,)),
            smem((N_EXPERTS,), lambda i: (0,)),
            pl.BlockSpec((tm, d), lambda i: (i, 0)),
            pl.BlockSpec(memory_space=pl.ANY),
        ],
        out_specs=pl.BlockSpec(memory_space=pl.ANY),
        out_shape=jax.ShapeDtypeStruct((n_rows, d), F32),
        scratch_shapes=[pltpu.SemaphoreType.DMA(())],
        input_output_aliases={4: 0},
        compiler_params=_params("arbitrary"),
        name="moe_dispatch",
    )(idx_flat, rank_flat, pad_start, h, jnp.zeros((n_rows, d), F32))


PERM_W = 256


def _deinterleave_kernel(w_ref, p_ref, g_ref, u_ref):
    half = PERM_W // 2
    for j in range(w_ref.shape[1] // PERM_W):
        r = _dot(w_ref[:, j * PERM_W:(j + 1) * PERM_W].astype(BF16), p_ref[...])
        g_ref[:, j * half:(j + 1) * half] = r[:, 0:half].astype(BF16)
        u_ref[:, j * half:(j + 1) * half] = r[:, half:PERM_W].astype(BF16)


def _deinterleave_gate_up(w_gate_up):
    depth, n_e, d, two_f = w_gate_up.shape
    half = PERM_W // 2
    perm = np.zeros((PERM_W, PERM_W), np.float32)
    for col in range(half):
        perm[2 * col, col] = 1.0
        perm[2 * col + 1, half + col] = 1.0
    out = jax.ShapeDtypeStruct((depth * n_e, d, two_f // 2), BF16)
    wg, wu = pl.pallas_call(
        _deinterleave_kernel,
        grid=(depth * n_e,),
        in_specs=[pl.BlockSpec((None, d, two_f), lambda i: (i, 0, 0)), _const_spec((PERM_W, PERM_W))],
        out_specs=[pl.BlockSpec((None, d, two_f // 2), lambda i: (i, 0, 0))] * 2,
        out_shape=[out, out],
        compiler_params=_params("arbitrary"),
        name="deinterleave_gate_up",
    )(w_gate_up.reshape(depth * n_e, d, two_f), jnp.asarray(perm, BF16))
    shape = (depth, n_e, d, two_f // 2)
    return wg.reshape(shape), wu.reshape(shape)


def _experts_kernel(be_ref, nact_ref, x_ref, wg_ref, wu_ref, bg_ref, bu_ref, wdn_ref, bdn_ref, o_ref):
    del be_ref
    active = pl.program_id(0) < nact_ref[0]

    @pl.when(jnp.logical_not(active))
    def _():
        o_ref[...] = jnp.zeros_like(o_ref)

    @pl.when(active)
    def _():
        xb = x_ref[...].astype(BF16)
        gate = jnp.minimum(_dot(xb, wg_ref[...]) + bg_ref[...], SWIGLU_LIMIT)
        up = jnp.clip(_dot(xb, wu_ref[...]) + bu_ref[...], -SWIGLU_LIMIT, SWIGLU_LIMIT)
        act = (up + 1.0) * gate * _sigmoid(SWIGLU_ALPHA * gate)
        o_ref[...] = _dot(act.astype(BF16), wdn_ref[...]) + bdn_ref[...]


def _experts(block_e, n_active, x_rows, wg, wu, bg, bu, wdn, bdn, layer):
    n_rows, d = x_rows.shape
    nb = n_rows // MOE_BLOCK

    def row_map(i, be, na):
        return (jnp.minimum(i, na[0] - 1), 0)

    def w_map(i, be, na):
        return (layer, be[i], 0, 0)

    grid_spec = pltpu.PrefetchScalarGridSpec(
        num_scalar_prefetch=2,
        grid=(nb,),
        in_specs=[
            pl.BlockSpec((MOE_BLOCK, d), row_map),
            pl.BlockSpec((None, None, d, D_EXPERT), w_map),
            pl.BlockSpec((None, None, d, D_EXPERT), w_map),
            pl.BlockSpec((None, None, 1, D_EXPERT), w_map),
            pl.BlockSpec((None, None, 1, D_EXPERT), w_map),
            pl.BlockSpec((None, None, D_EXPERT, d), w_map),
            pl.BlockSpec((None, None, 1, d), w_map),
        ],
        out_specs=pl.BlockSpec((MOE_BLOCK, d), lambda i, be, na: (i, 0)),
    )
    return pl.pallas_call(
        _experts_kernel,
        grid_spec=grid_spec,
        out_shape=jax.ShapeDtypeStruct((n_rows, d), F32),
        compiler_params=_params("arbitrary"),
        name="moe_experts",
    )(block_e, n_active, x_rows, wg, wu, bg, bu, wdn, bdn)


def _combine_kernel(idx_ref, rank_ref, start_ref, y_hbm, w_ref, x_ref, mod_ref, fin_ref,
                    o_ref, buf, sem, *, final_norm):
    tm = x_ref.shape[0]

    def issue(r, carry):
        for kk in range(TOP_K):
            a = r * TOP_K + kk
            src = start_ref[idx_ref[a]] + rank_ref[a]
            _row_copy(y_hbm, src, buf.at[kk], r, sem).start(priority=kk % 2)
        return carry

    lax.fori_loop(0, tm, issue, 0, unroll=DMA_UNROLL)

    def drain(r, carry):
        for _ in range(TOP_K):
            _row_copy(y_hbm, 0, buf.at[0], 0, sem).wait()
        return carry

    lax.fori_loop(0, tm, drain, 0, unroll=DMA_UNROLL)

    w = w_ref[...]
    moe = w[:, 0:1] * buf[0]
    for kk in range(1, TOP_K):
        moe = moe + w[:, kk:kk + 1] * buf[kk]
    out = x_ref[...] + mod_ref[5:6, :] * moe
    if final_norm:
        ms = jnp.mean(out * out, axis=-1, keepdims=True)
        out = out * lax.rsqrt(ms + NORM_EPS) * fin_ref[...]
    o_ref[...] = out


def _combine(idx_flat, rank_flat, pad_start, y_rows, w, x, mod, fin_gain, seq, final_norm):
    t, d = x.shape
    tm = min(256, seq)
    per_b = seq // tm
    na = tm * TOP_K
    smem = functools.partial(pl.BlockSpec, memory_space=pltpu.SMEM)
    row = lambda i: (i, 0)
    return pl.pallas_call(
        functools.partial(_combine_kernel, final_norm=final_norm),
        grid=(t // tm,),
        in_specs=[
            smem((na,), lambda i: (i,)), smem((na,), lambda i: (i,)),
            smem((N_EXPERTS,), lambda i: (0,)),
            pl.BlockSpec(memory_space=pl.ANY),
            pl.BlockSpec((tm, TOP_K), row), pl.BlockSpec((tm, d), row),
            pl.BlockSpec((None, 6, d), lambda i: (i // per_b, 0, 0)),
            _const_spec((1, d)),
        ],
        out_specs=pl.BlockSpec((tm, d), row),
        out_shape=jax.ShapeDtypeStruct((t, d), F32),
        scratch_shapes=[pltpu.VMEM((TOP_K, tm, d), F32), pltpu.SemaphoreType.DMA(())],
        compiler_params=_params("arbitrary"),
        name="moe_combine",
    )(idx_flat, rank_flat, pad_start, y_rows, w, x, mod, fin_gain)


def _rope_tables(positions):
    inv_freq = ROPE_THETA ** (-jnp.arange(0, ROPE, 2, dtype=F32) / ROPE)
    ang = positions.astype(F32).reshape(-1)[:, None] * inv_freq
    cos, sin = jnp.cos(ang), jnp.sin(ang)
    t = cos.shape[0]
    pad = jnp.zeros((t, HEAD_LANES - NOPE - ROPE), F32)
    cos_t = jnp.concatenate([jnp.ones((t, NOPE), F32), cos, cos, pad], axis=1)
    sin_t = jnp.concatenate([jnp.zeros((t, NOPE), F32), sin, sin, pad], axis=1)
    return cos_t, sin_t


def _rotate_half_matrix():
    half = ROPE // 2
    r = np.zeros((ROPE, ROPE), np.float32)
    for i in range(half):
        r[half + i, i] = -1.0
        r[i, half + i] = 1.0
    return r


def _mla_weights(w_q_up, w_kv_up):
    rot = jnp.asarray(_rotate_half_matrix())
    wq = w_q_up.reshape(Q_RANK, MLA_HEADS, NOPE + ROPE)
    zq = jnp.zeros((Q_RANK, MLA_HEADS, HEAD_LANES - NOPE - ROPE), F32)
    wq1 = jnp.concatenate([wq, zq], axis=2)
    wq_rot = jnp.einsum("rhd,de->rhe", wq[:, :, NOPE:], rot)
    wq2 = jnp.concatenate([jnp.zeros((Q_RANK, MLA_HEADS, NOPE), F32), wq_rot, zq], axis=2)
    wkv = w_kv_up.reshape(KV_RANK, MLA_HEADS, NOPE + MLA_DV)
    wk = jnp.concatenate([wkv[:, :, :NOPE], jnp.zeros((KV_RANK, MLA_HEADS, HEAD_LANES - NOPE), F32)], axis=2)
    wv = wkv[:, :, NOPE:]
    qk_w = MLA_HEADS * HEAD_LANES
    return (wq1.reshape(Q_RANK, qk_w).astype(BF16), wq2.reshape(Q_RANK, qk_w).astype(BF16),
            wk.reshape(KV_RANK, qk_w).astype(BF16), wv.reshape(KV_RANK, MLA_VW).astype(BF16))


def _rope_placement():
    rot = _rotate_half_matrix()
    e1 = np.zeros((LANES, MLA_HEADS, HEAD_LANES), np.float32)
    e2 = np.zeros((LANES, MLA_HEADS, HEAD_LANES), np.float32)
    for h in range(MLA_HEADS):
        e1[:ROPE, h, NOPE:NOPE + ROPE] = np.eye(ROPE, dtype=np.float32)
        e2[:ROPE, h, NOPE:NOPE + ROPE] = rot
    shape = (LANES, MLA_HEADS * HEAD_LANES)
    return jnp.asarray(e1.reshape(shape), BF16), jnp.asarray(e2.reshape(shape), BF16)


def _in_proj_weights(w_in_l):
    hg = w_in_l[:, 0:5 * HG_W]
    o = 5 * HG_W
    ml = w_in_l[:, o:o + Q_RANK + KV_RANK + ROPE]
    ml = jnp.concatenate([ml, jnp.zeros((D_MODEL, MLA_IN_W - ml.shape[1]), F32)], axis=1)
    gt = w_in_l[:, o + Q_RANK + KV_RANK + ROPE:]
    return hg.astype(BF16), ml.astype(BF16), gt.astype(BF16)


def kernel(x, c, positions, w_ada, b_ada, norm_mix_gain, norm_ffn_gain, w_in, hgrn_lb_logits,
           hgrn_out_norm_gain, w_out_hgrn, mla_q_norm_gain, w_q_up, mla_kv_norm_gain, w_kv_up,
           w_out_mla, w_o, w_router, b_router, w_gate_up, b_gate_up, w_down, b_down,
           final_norm_gain):
    batch, seq, d = x.shape
    depth = w_ada.shape[0]
    t = batch * seq
    n_rows = _moe_num_blocks(t) * MOE_BLOCK

    mod_all = _ada_modulation(c, w_ada, b_ada).reshape(depth, batch, 6, d)
    lb_all = _hgrn_lower_bounds(hgrn_lb_logits)
    cos_t, sin_t = _rope_tables(positions)
    e1, e2 = _rope_placement()

    wg, wu = _deinterleave_gate_up(w_gate_up)
    bg = b_gate_up[..., 0::2][:, :, None, :]
    bu = b_gate_up[..., 1::2][:, :, None, :]
    wdn = w_down.astype(BF16)
    bdn = b_down[:, :, None, :]
    wr_pad = jnp.pad(w_router, ((0, 0), (0, 0), (0, LANES - N_EXPERTS)))
    br_pad = jnp.pad(b_router, ((0, 0), (0, LANES - N_EXPERTS)))

    xs = x.reshape(t, d)
    for l in range(depth):
        mod = mod_all[l]
        whg, wml, wgt = _in_proj_weights(w_in[l])
        q, lff, lfb, v, gs, m, gt = _mix_in(xs, mod, norm_mix_gain[l][None, :], lb_all[l], whg, wml, wgt, seq)
        ho = _hgrn(q, lff, lfb, v, gs, hgrn_out_norm_gain[l][None, :], batch, seq)
        wq1, wq2, wk, wv = _mla_weights(w_q_up[l], w_kv_up[l])
        qh, kh, vh = _mla_prep(m, cos_t, sin_t, mla_q_norm_gain[l][None, :], mla_kv_norm_gain[l][None, :],
                               wq1, wq2, wk, wv, e1, e2, seq)
        mo = _mla_attn(qh, kh, vh, batch, seq)
        xs = _mix_out(ho, mo, gt, xs, mod, w_out_hgrn[l].astype(BF16), w_out_mla[l].astype(BF16),
                      w_o[l].astype(BF16), seq)

        wr_hi = wr_pad[l].astype(BF16)
        wr_lo = (wr_pad[l] - wr_hi.astype(F32)).astype(BF16)
        h2, idx, tw, rank, meta, block_e = _ffn_prep(xs, mod, norm_ffn_gain[l][None, :], wr_hi, wr_lo,
                                                     br_pad[l][None, :], seq)
        pad_start = meta[0, :N_EXPERTS]
        n_active = meta[1, N_EXPERTS - 1:N_EXPERTS] // MOE_BLOCK
        idx_flat = idx.reshape(-1)
        rank_flat = rank.reshape(-1)
        x_rows = _dispatch(idx_flat, rank_flat, pad_start, h2, n_rows)
        y_rows = _experts(block_e[:n_rows // MOE_BLOCK, 0], n_active, x_rows, wg, wu, bg, bu, wdn, bdn, l)
        xs = _combine(idx_flat, rank_flat, pad_start, y_rows, tw, xs, mod, final_norm_gain[None, :],
                      seq, final_norm=(l == depth - 1))
    return xs.reshape(batch, seq, d)
```
